```python
import jax, jax.numpy as jnp
from jax import lax
import numpy as np

D_MODEL = 2048
BATCH = 4
SEQ = 4096
DEPTH = 4
DEC_BATCH = 8
DEC_SEQ = 4096
PAST_LEN = 128

N_MIXERS = 2
N_A = (DEPTH + 1) // 2
N_B = DEPTH // 2
MIX_WIDTH = D_MODEL
N_MIX_HEADS = 16
HEAD_W = MIX_WIDTH // N_MIX_HEADS
CONV_W = 4
LRU_C = 8.0
CHUNK = 128
N_MEM = 256
XA_HEADS = 4
XA_HEAD_DIM = 128
XA_WIDTH = XA_HEADS * XA_HEAD_DIM
N_GROUPS = 8
EXPERTS_PER_GROUP = 8
N_EXPERTS = N_GROUPS * EXPERTS_PER_GROUP
TOP_K = 2
EXPERT_HIDDEN = 1024
MOE_BLOCK = 128
EPS = 1e-6

kernel_name = 'hybrid_rglru_gmlp_hmoe_encoder'


def _rms_norm(x, g):
    xf = x.astype(jnp.float32)
    y = xf * lax.rsqrt(jnp.mean(xf * xf, axis=-1, keepdims=True) + EPS)
    return (y * g.astype(jnp.float32)).astype(x.dtype)


def _linear_scan(a, b, reverse):
    def comb(l, r):
        a1, b1 = l
        a2, b2 = r
        return a1 * a2, a2 * b1 + b2
    _, h = lax.associative_scan(comb, (a, b), axis=1, reverse=reverse)
    return h


def _rglru_mixer(h, w_in, w_out, conv_w, conv_b, wa, ba, wx, bx, lam):
    bn, s, _ = h.shape
    z = h @ w_in
    gate = jax.nn.gelu(z[..., :MIX_WIDTH])
    xr = z[..., MIX_WIDTH:]
    pad_l = CONV_W // 2
    xp = jnp.pad(xr, ((0, 0), (pad_l, CONV_W - 1 - pad_l), (0, 0)))
    xc = conv_b + sum(xp[:, k:k + s] * conv_w[k] for k in range(CONV_W))
    xh = xc.reshape(bn, s, N_MIX_HEADS, HEAD_W)
    y = jnp.zeros((bn, s, MIX_WIDTH), jnp.float32)
    for d, rev in ((0, False), (1, True)):
        r = jax.nn.sigmoid(jnp.einsum('bshi,hij->bshj', xh, wa[d]).reshape(bn, s, MIX_WIDTH) + ba[d])
        i = jax.nn.sigmoid(jnp.einsum('bshi,hij->bshj', xh, wx[d]).reshape(bn, s, MIX_WIDTH) + bx[d])
        log_a = -LRU_C * r.astype(jnp.float32) * jax.nn.softplus(-lam[d].astype(jnp.float32))
        a = jnp.exp(log_a)
        b = jnp.sqrt(-jnp.expm1(2.0 * log_a)) * (i * xc).astype(jnp.float32)
        y = y + _linear_scan(a, b, rev)
    y = gate * y.astype(h.dtype)
    return y @ w_out


def _gmlp_mixer(h, w_in, w_out, sg_norm, sg_w, sg_b):
    bn, s, _ = h.shape
    z = jax.nn.gelu(h @ w_in)
    u = z[..., :MIX_WIDTH]
    v = _rms_norm(z[..., MIX_WIDTH:], sg_norm)
    vc = v.reshape(bn, s // CHUNK, CHUNK, N_MIX_HEADS, HEAD_W)
    mixed = jnp.einsum('hpq,bnqhc->bnphc', sg_w, vc) + sg_b.T[:, :, None]
    y = u * mixed.reshape(bn, s, MIX_WIDTH)
    return y @ w_out


def _mem_cross_attn(h, mem_n, wq, wkv, wo):
    bn, s, _ = h.shape
    q = (h @ wq).reshape(bn, s, XA_HEADS, XA_HEAD_DIM)
    kv = (mem_n @ wkv).reshape(bn, N_MEM, 2, XA_HEADS, XA_HEAD_DIM)
    k, v = kv[:, :, 0], kv[:, :, 1]
    sc = jnp.einsum('bshd,bmhd->bhsm', q, k).astype(jnp.float32) * (XA_HEAD_DIM ** -0.5)
    p = jax.nn.softmax(sc, axis=-1).astype(h.dtype)
    o = jnp.einsum('bhsm,bmhd->bshd', p, v).reshape(bn, s, XA_WIDTH)
    return o @ wo


def _hier_moe(h, wg, bg, we, be, w_up, w_down):
    shp = h.shape
    x2 = h.reshape(-1, D_MODEL)
    t = x2.shape[0]
    xf = x2.astype(jnp.float32)
    pg = jax.nn.softmax(xf @ wg.astype(jnp.float32) + bg.astype(jnp.float32), axis=-1)
    g_sel = jnp.argmax(pg, axis=-1)
    gate_g = jnp.take_along_axis(pg, g_sel[:, None], axis=1)[:, 0]
    le = (xf @ we.astype(jnp.float32) + be.astype(jnp.float32)).reshape(t, N_GROUPS, EXPERTS_PER_GROUP)
    le_sel = jnp.take_along_axis(le, g_sel[:, None, None], axis=1)[:, 0]
    pe = jax.nn.softmax(le_sel, axis=-1)
    top_w, top_i = lax.top_k(pe, TOP_K)
    top_w = top_w / jnp.sum(top_w, axis=-1, keepdims=True)
    gates = gate_g[:, None] * top_w
    expert = g_sel[:, None] * EXPERTS_PER_GROUP + top_i
    tk = t * TOP_K
    flat_e = expert.reshape(-1)
    flat_tok = jnp.repeat(jnp.arange(t), TOP_K)
    flat_g = gates.reshape(-1)
    counts = jnp.bincount(flat_e, length=N_EXPERTS)
    padded = ((counts + MOE_BLOCK - 1) // MOE_BLOCK) * MOE_BLOCK
    pad_end = jnp.cumsum(padded)
    pad_start = pad_end - padded
    start = jnp.cumsum(counts) - counts
    order = jnp.argsort(flat_e, stable=True)
    se = flat_e[order]
    dest = pad_start[se] + (jnp.arange(tk) - start[se])
    n_blocks = (tk + MOE_BLOCK - 1) // MOE_BLOCK + N_EXPERTS
    n_rows = n_blocks * MOE_BLOCK
    row_tok = jnp.full((n_rows,), t, jnp.int32).at[dest].set(flat_tok[order].astype(jnp.int32))
    row_gate = jnp.zeros((n_rows,), x2.dtype).at[dest].set(flat_g[order].astype(x2.dtype))
    block_e = jnp.clip(jnp.searchsorted(pad_end, jnp.arange(n_blocks) * MOE_BLOCK, side='right'), 0, N_EXPERTS - 1)
    xpad = jnp.concatenate([x2, jnp.zeros((1, D_MODEL), x2.dtype)], axis=0)

    def expert_block(args):
        rows, e = args
        xb = xpad[rows]
        hu = xb @ w_up[e]
        hh = jax.nn.silu(hu[:, :EXPERT_HIDDEN]) * hu[:, EXPERT_HIDDEN:]
        return hh @ w_down[e]

    out = lax.map(expert_block, (row_tok.reshape(n_blocks, MOE_BLOCK), block_e))
    out = out.reshape(n_rows, D_MODEL) * row_gate[:, None]
    y = jnp.zeros((t + 1, D_MODEL), x2.dtype).at[row_tok].add(out)[:t]
    return y.reshape(shp)


def _trunk(x, mem, g_mix, w_in, w_out, conv_w, conv_b, lru_wa, lru_ba, lru_wx, lru_bx, lru_lambda,
           sg_norm, sg_w, sg_b, g_xattn, g_mem, xa_wq, xa_wkv, xa_wo,
           g_moe, router_wg, router_bg, router_we, router_be, moe_w_up, moe_w_down, g_final):
    for i in range(DEPTH):
        h = _rms_norm(x, g_mix[i])
        j = i // N_MIXERS
        if i % N_MIXERS == 0:
            x = x + _rglru_mixer(h, w_in[i], w_out[i], conv_w[j], conv_b[j], lru_wa[j], lru_ba[j],
                                 lru_wx[j], lru_bx[j], lru_lambda[j])
        else:
            x = x + _gmlp_mixer(h, w_in[i], w_out[i], sg_norm[j], sg_w[j], sg_b[j])
        x = x + _mem_cross_attn(_rms_norm(x, g_xattn[i]), _rms_norm(mem, g_mem[i]), xa_wq[i], xa_wkv[i], xa_wo[i])
        x = x + _hier_moe(_rms_norm(x, g_moe[i]), router_wg[i], router_bg[i], router_we[i], router_be[i],
                          moe_w_up[i], moe_w_down[i])
    return _rms_norm(x, g_final)


def setup_inputs(seed: int = 0) -> dict:
    key = jax.random.key(seed)
    ks = jax.random.split(key, 32)
    f32 = jnp.float32
    nrm = lambda k, shape, scale: jax.random.normal(k, shape, f32) * scale
    u = jax.random.uniform(ks[12], (N_A, 2, MIX_WIDTH), f32, minval=0.9, maxval=0.999)
    sl = u ** (1.0 / LRU_C)
    return {
        'x_prompt': nrm(ks[0], (BATCH, SEQ, D_MODEL), 1.0),
        'x_sample': nrm(ks[1], (DEC_BATCH, DEC_SEQ, D_MODEL), 1.0),
        'mem_prompt': nrm(ks[2], (BATCH, N_MEM, D_MODEL), 1.0),
        'mem_sample': nrm(ks[3], (DEC_BATCH, N_MEM, D_MODEL), 1.0),
        'g_mix': 1.0 + nrm(ks[4], (DEPTH, D_MODEL), 0.02),
        'w_in': nrm(ks[5], (DEPTH, D_MODEL, 2 * MIX_WIDTH), D_MODEL ** -0.5),
        'w_out': nrm(ks[6], (DEPTH, MIX_WIDTH, D_MODEL), MIX_WIDTH ** -0.5),
        'conv_w': nrm(ks[7], (N_A, CONV_W, MIX_WIDTH), CONV_W ** -0.5),
        'conv_b': nrm(ks[8], (N_A, MIX_WIDTH), 0.02),
        'lru_wa': nrm(ks[9], (N_A, 2, N_MIX_HEADS, HEAD_W, HEAD_W), HEAD_W ** -0.5),
        'lru_ba': nrm(ks[10], (N_A, 2, MIX_WIDTH), 0.1),
        'lru_wx': nrm(ks[11], (N_A, 2, N_MIX_HEADS, HEAD_W, HEAD_W), HEAD_W ** -0.5),
        'lru_bx': nrm(ks[13], (N_A, 2, MIX_WIDTH), 0.1),
        'lru_lambda': jnp.log(sl) - jnp.log1p(-sl),
        'sg_norm': 1.0 + nrm(ks[14], (N_B, MIX_WIDTH), 0.02),
        'sg_w': nrm(ks[15], (N_B, N_MIX_HEADS, CHUNK, CHUNK), CHUNK ** -0.5),
        'sg_b': 1.0 + nrm(ks[16], (N_B, N_MIX_HEADS, CHUNK), 0.1),
        'g_xattn': 1.0 + nrm(ks[17], (DEPTH, D_MODEL), 0.02),
        'g_mem': 1.0 + nrm(ks[18], (DEPTH, D_MODEL), 0.02),
        'xa_wq': nrm(ks[19], (DEPTH, D_MODEL, XA_WIDTH), D_MODEL ** -0.5),
        'xa_wkv': nrm(ks[20], (DEPTH, D_MODEL, 2 * XA_WIDTH), D_MODEL ** -0.5),
        'xa_wo': nrm(ks[21], (DEPTH, XA_WIDTH, D_MODEL), XA_WIDTH ** -0.5),
        'g_moe': 1.0 + nrm(ks[22], (DEPTH, D_MODEL), 0.02),
        'router_wg': nrm(ks[23], (DEPTH, D_MODEL, N_GROUPS), D_MODEL ** -0.5),
        'router_bg': nrm(ks[24], (DEPTH, N_GROUPS), 0.01),
        'router_we': nrm(ks[25], (DEPTH, D_MODEL, N_EXPERTS), D_MODEL ** -0.5),
        'router_be': nrm(ks[26], (DEPTH, N_EXPERTS), 0.01),
        'moe_w_up': nrm(ks[27], (DEPTH, N_EXPERTS, D_MODEL, 2 * EXPERT_HIDDEN), D_MODEL ** -0.5),
        'moe_w_down': nrm(ks[28], (DEPTH, N_EXPERTS, EXPERT_HIDDEN, D_MODEL), EXPERT_HIDDEN ** -0.5),
        'g_final': 1.0 + nrm(ks[29], (D_MODEL,), 0.02),
    }


def reference(x_prompt, x_sample, mem_prompt, mem_sample, g_mix, w_in, w_out, conv_w, conv_b,
              lru_wa, lru_ba, lru_wx, lru_bx, lru_lambda, sg_norm, sg_w, sg_b,
              g_xattn, g_mem, xa_wq, xa_wkv, xa_wo, g_moe, router_wg, router_bg, router_we, router_be,
              moe_w_up, moe_w_down, g_final):
    y_prompt = _trunk(x_prompt, mem_prompt, g_mix, w_in, w_out, conv_w, conv_b, lru_wa, lru_ba, lru_wx, lru_bx,
                      lru_lambda, sg_norm, sg_w, sg_b, g_xattn, g_mem, xa_wq, xa_wkv, xa_wo, g_moe,
                      router_wg, router_bg, router_we, router_be, moe_w_up, moe_w_down, g_final)
    y_sample = _trunk(x_sample, mem_sample, g_mix, w_in, w_out, conv_w, conv_b, lru_wa, lru_ba, lru_wx, lru_bx,
                      lru_lambda, sg_norm, sg_w, sg_b, g_xattn, g_mem, xa_wq, xa_wkv, xa_wo, g_moe,
                      router_wg, router_bg, router_we, router_be, moe_w_up, moe_w_down, g_final)
    return (y_prompt, y_sample)
```

```python
import functools

import jax
import jax.numpy as jnp
from jax import lax
from jax.experimental import pallas as pl
from jax.experimental.pallas import tpu as pltpu

F32 = jnp.float32
BF16 = jnp.bfloat16
U32 = jnp.uint32
I32 = jnp.int32

EPS = 1e-6
LRU_C = 8.0
HEAD_W = 128
CHUNK = 128
CONV_W = 4
XA_HEADS = 4
XA_HEAD_DIM = 128
N_GROUPS = 8
EXPERTS_PER_GROUP = 8
N_EXPERTS = 64
ROUTE_LANES = 128
EXP_LANE0 = N_GROUPS

VMEM_LIMIT_BYTES = 56 * 1024 * 1024
SUBLANES = 8

MOE_BM = 256
HI_MASK = 0xFFFF0000


def _cparams(semantics):
    return pltpu.CompilerParams(dimension_semantics=semantics, vmem_limit_bytes=VMEM_LIMIT_BYTES)


def _gelu(x):
    return 0.5 * x * (1.0 + jnp.tanh(0.7978845608028654 * (x + 0.044715 * (x * x * x))))


def _rms(x, g):
    ms = jnp.mean(x * x, axis=-1, keepdims=True)
    return x * lax.rsqrt(ms + EPS) * g


def _pack_bf16_pairs(x):
    n = x.shape[1] // 2
    bits = lax.bitcast_convert_type(x.astype(BF16).astype(F32), U32)
    return (bits[:, :n] >> 16) | (bits[:, n:] & jnp.uint32(HI_MASK))


def _unpack_lo(w):
    return lax.bitcast_convert_type(w << 16, F32)


def _unpack_hi(w):
    return lax.bitcast_convert_type(w & jnp.uint32(HI_MASK), F32)


def _norm_matmul_body(x_ref, g_ref, w_ref, o_ref, h_ref, *, n_gelu):
    j = pl.program_id(1)

    @pl.when(j == 0)
    def _():
        h_ref[...] = _rms(x_ref[...], g_ref[...]).astype(BF16)

    def emit(act):
        z = jnp.dot(h_ref[...], w_ref[...], preferred_element_type=F32)
        o_ref[...] = (_gelu(z) if act else z).astype(o_ref.dtype)

    nj = pl.num_programs(1)
    if n_gelu <= 0:
        emit(False)
    else:
        @pl.when(j < n_gelu)
        def _():
            emit(True)

        @pl.when(j >= n_gelu)
        def _():
            emit(False)
    del nj


def _norm_matmul(x, g, w, *, tm, tn, gelu_cols, out_dtype, name):
    t, d = x.shape
    n = w.shape[1]
    tm = min(tm, t)
    tn = min(tn, n)
    assert t % tm == 0 and n % tn == 0 and gelu_cols % tn == 0
    return pl.pallas_call(
        functools.partial(_norm_matmul_body, n_gelu=gelu_cols // tn),
        grid=(t // tm, n // tn),
        in_specs=[
            pl.BlockSpec((tm, d), lambda i, j: (i, 0)),
            pl.BlockSpec((1, d), lambda i, j: (0, 0)),
            pl.BlockSpec((d, tn), lambda i, j: (0, j)),
        ],
        out_specs=pl.BlockSpec((tm, tn), lambda i, j: (i, j)),
        out_shape=jax.ShapeDtypeStruct((t, n), out_dtype),
        scratch_shapes=[pltpu.VMEM((tm, d), BF16)],
        compiler_params=_cparams(("parallel", "arbitrary")),
        name=name,
    )(x, g, w)


def _res_matmul_body(y_ref, w_ref, x_ref, o_ref):
    o_ref[...] = x_ref[...] + jnp.dot(y_ref[...], w_ref[...], preferred_element_type=F32)


def _res_matmul(y, w, x, *, tm, tn, name):
    t, k = y.shape
    n = w.shape[1]
    tm = min(tm, t)
    assert t % tm == 0 and n % tn == 0
    return pl.pallas_call(
        _res_matmul_body,
        grid=(t // tm, n // tn),
        in_specs=[
            pl.BlockSpec((tm, k), lambda i, j: (i, 0)),
            pl.BlockSpec((k, tn), lambda i, j: (0, j)),
            pl.BlockSpec((tm, tn), lambda i, j: (i, j)),
        ],
        out_specs=pl.BlockSpec((tm, tn), lambda i, j: (i, j)),
        out_shape=jax.ShapeDtypeStruct((t, n), F32),
        compiler_params=_cparams(("parallel", "parallel")),
        name=name,
    )(y, w, x)


LRU_CW = 256
LRU_ROWS = 256
LRU_PAD = 8


def _lru_body(zg_ref, zx_ref, cw_ref, cb_ref, w_ref, b_ref, lam_ref, o_ref,
              xf_ref, af_ref, bf_ref, ab_ref, bb_ref, *, seq):
    cw = LRU_CW
    rows = LRU_ROWS
    zero_pad = jnp.zeros((LRU_PAD, cw), F32)
    xf_ref[0:LRU_PAD, :] = zero_pad
    xf_ref[LRU_PAD + seq:LRU_PAD + seq + LRU_PAD, :] = zero_pad
    xf_ref[LRU_PAD:LRU_PAD + seq, :] = zx_ref[...].astype(F32)

    nl = -lam_ref[...]
    softplus = jnp.maximum(nl, 0.0) + jnp.log1p(jnp.exp(-jnp.abs(nl)))
    cwv = cw_ref[...]
    cbv = cb_ref[...]
    dirs = ((af_ref, bf_ref), (ab_ref, bb_ref))

    def gate_chunk(c, carry):
        t0 = pl.multiple_of(c * rows, rows)
        xw = xf_ref[pl.ds(t0, rows + 2 * LRU_PAD), :]
        xc = cbv
        for k in range(CONV_W):
            off = LRU_PAD + k - CONV_W // 2
            xc = xc + xw[off:off + rows, :] * cwv[k:k + 1, :]
        for hh in range(cw // HEAD_W):
            sl = slice(hh * HEAD_W, (hh + 1) * HEAD_W)
            xch = xc[:, sl]
            gates = jnp.dot(xch.astype(BF16), w_ref[hh], preferred_element_type=F32) + b_ref[hh]
            for d, (a_ref, b_ref_d) in enumerate(dirs):
                r = jax.nn.sigmoid(gates[:, (2 * d) * HEAD_W:(2 * d + 1) * HEAD_W])
                i = jax.nn.sigmoid(gates[:, (2 * d + 1) * HEAD_W:(2 * d + 2) * HEAD_W])
                a = jnp.exp((-LRU_C) * r * softplus[d:d + 1, sl])
                a_ref[pl.ds(t0, rows), sl] = a
                b_ref_d[pl.ds(t0, rows), sl] = jnp.sqrt(1.0 - a * a) * (i * xch)
        return carry

    lax.fori_loop(0, seq // rows, gate_chunk, 0)

    row = lax.broadcasted_iota(I32, (SUBLANES, cw), 0)
    n_tiles = seq // SUBLANES

    def scan_tile(k, carry):
        cf, cb = carry
        tf = pl.multiple_of(k * SUBLANES, SUBLANES)
        a = af_ref[pl.ds(tf, SUBLANES), :]
        b = bf_ref[pl.ds(tf, SUBLANES), :]
        for s in (1, 2, 4):
            m = row >= s
            b = jnp.where(m, a * pltpu.roll(b, s, 0) + b, b)
            a = jnp.where(m, a * pltpu.roll(a, s, 0), a)
        h = a * cf + b
        bf_ref[pl.ds(tf, SUBLANES), :] = h
        cf = jnp.broadcast_to(h[SUBLANES - 1:SUBLANES, :], (SUBLANES, cw))
        tb = pl.multiple_of((n_tiles - 1 - k) * SUBLANES, SUBLANES)
        a = ab_ref[pl.ds(tb, SUBLANES), :]
        b = bb_ref[pl.ds(tb, SUBLANES), :]
        for s in (1, 2, 4):
            m = row < SUBLANES - s
            b = jnp.where(m, a * pltpu.roll(b, SUBLANES - s, 0) + b, b)
            a = jnp.where(m, a * pltpu.roll(a, SUBLANES - s, 0), a)
        h = a * cb + b
        bb_ref[pl.ds(tb, SUBLANES), :] = h
        cb = jnp.broadcast_to(h[0:1, :], (SUBLANES, cw))
        return cf, cb

    zero_tile = jnp.zeros((SUBLANES, cw), F32)
    lax.fori_loop(0, n_tiles, scan_tile, (zero_tile, zero_tile), unroll=4)

    def out_chunk(c, carry):
        t0 = pl.multiple_of(c * rows, rows)
        y = zg_ref[pl.ds(t0, rows), :].astype(F32) * (bf_ref[pl.ds(t0, rows), :] + bb_ref[pl.ds(t0, rows), :])
        o_ref[pl.ds(t0, rows), :] = y.astype(BF16)
        return carry

    lax.fori_loop(0, seq // rows, out_chunk, 0)


def _lru_core(z, conv_w, conv_b, wcat, bcat, lam, *, n_batch, seq, name):
    t, two_w = z.shape
    width = two_w // 2
    cw = LRU_CW
    n_ct = width // cw
    hpt = cw // HEAD_W
    assert seq % LRU_ROWS == 0 and width % cw == 0
    return pl.pallas_call(
        functools.partial(_lru_body, seq=seq),
        grid=(n_batch, n_ct),
        in_specs=[
            pl.BlockSpec((seq, cw), lambda b, c: (b, c)),
            pl.BlockSpec((seq, cw), lambda b, c: (b, n_ct + c)),
            pl.BlockSpec((CONV_W, cw), lambda b, c: (0, c)),
            pl.BlockSpec((1, cw), lambda b, c: (0, c)),
            pl.BlockSpec((hpt, HEAD_W, 4 * HEAD_W), lambda b, c: (c, 0, 0)),
            pl.BlockSpec((hpt, 1, 4 * HEAD_W), lambda b, c: (c, 0, 0)),
            pl.BlockSpec((2, cw), lambda b, c: (0, c)),
        ],
        out_specs=pl.BlockSpec((seq, cw), lambda b, c: (b, c)),
        out_shape=jax.ShapeDtypeStruct((t, width), BF16),
        scratch_shapes=[
            pltpu.VMEM((seq + 2 * LRU_PAD, cw), F32),
            pltpu.VMEM((seq, cw), F32),
            pltpu.VMEM((seq, cw), F32),
            pltpu.VMEM((seq, cw), F32),
            pltpu.VMEM((seq, cw), F32),
        ],
        compiler_params=_cparams(("parallel", "parallel")),
        name=name,
    )(z, z, conv_w, conv_b, wcat, bcat, lam)


def _gmlp_body(z_ref, x_ref, sgn_ref, sgw_ref, sgbt_ref, w_ref, o_ref, vn_ref, y_ref, *, tm, width):
    v = z_ref[:, width:].astype(F32)
    vn_ref[...] = _rms(v, sgn_ref[...]).astype(BF16)
    for n in range(tm // CHUNK):
        rs = slice(n * CHUNK, (n + 1) * CHUNK)
        for h in range(width // HEAD_W):
            cs = slice(h * HEAD_W, (h + 1) * HEAD_W)
            mixed = jnp.dot(sgw_ref[h], vn_ref[rs, cs], preferred_element_type=F32) + sgbt_ref[:, h:h + 1]
            y_ref[rs, cs] = (z_ref[rs, cs].astype(F32) * mixed).astype(BF16)
    o_ref[...] = x_ref[...] + jnp.dot(y_ref[...], w_ref[...], preferred_element_type=F32)


def _gmlp_out(z, x, sg_norm, sg_w, sg_bt, w_out, *, tm, name):
    t, two_w = z.shape
    width = two_w // 2
    d = w_out.shape[1]
    n_heads = width // HEAD_W
    tm = min(tm, t)
    assert t % tm == 0 and tm % CHUNK == 0
    return pl.pallas_call(
        functools.partial(_gmlp_body, tm=tm, width=width),
        grid=(t // tm,),
        in_specs=[
            pl.BlockSpec((tm, two_w), lambda i: (i, 0)),
            pl.BlockSpec((tm, d), lambda i: (i, 0)),
            pl.BlockSpec((1, width), lambda i: (0, 0)),
            pl.BlockSpec((n_heads, CHUNK, CHUNK), lambda i: (0, 0, 0)),
            pl.BlockSpec((CHUNK, n_heads), lambda i: (0, 0)),
            pl.BlockSpec((width, d), lambda i: (0, 0)),
        ],
        out_specs=pl.BlockSpec((tm, d), lambda i: (i, 0)),
        out_shape=jax.ShapeDtypeStruct((t, d), F32),
        scratch_shapes=[pltpu.VMEM((tm, width), BF16), pltpu.VMEM((tm, width), BF16)],
        compiler_params=_cparams(("parallel",)),
        name=name,
    )(z, x, sg_norm, sg_w, sg_bt, w_out)


def _xattn_body(x_ref, g_ref, wq_ref, kv_ref, wo_ref, o_ref):
    x = x_ref[...]
    h = _rms(x, g_ref[...]).astype(BF16)
    q = jnp.dot(h, wq_ref[...], preferred_element_type=F32) * (XA_HEAD_DIM ** -0.5)
    qb = q.astype(BF16)
    xa_w = XA_HEADS * XA_HEAD_DIM
    outs = []
    for hd in range(XA_HEADS):
        cs = slice(hd * XA_HEAD_DIM, (hd + 1) * XA_HEAD_DIM)
        vs = slice(xa_w + hd * XA_HEAD_DIM, xa_w + (hd + 1) * XA_HEAD_DIM)
        s = lax.dot_general(qb[:, cs], kv_ref[:, cs], (((1,), (1,)), ((), ())), preferred_element_type=F32)
        p = jnp.exp(s - jnp.max(s, axis=-1, keepdims=True))
        l = jnp.sum(p, axis=-1, keepdims=True)
        o = jnp.dot(p.astype(BF16), kv_ref[:, vs], preferred_element_type=F32) / l
        outs.append(o.astype(BF16))
    o = jnp.concatenate(outs, axis=1)
    o_ref[...] = x + jnp.dot(o, wo_ref[...], preferred_element_type=F32)


def _xattn(x, g, wq, kv, wo, *, seq, n_mem, tm, name):
    t, d = x.shape
    xa_w = wq.shape[1]
    tm = min(tm, seq)
    assert seq % tm == 0
    tiles_per_seq = seq // tm
    return pl.pallas_call(
        _xattn_body,
        grid=(t // tm,),
        in_specs=[
            pl.BlockSpec((tm, d), lambda i: (i, 0)),
            pl.BlockSpec((1, d), lambda i: (0, 0)),
            pl.BlockSpec((d, xa_w), lambda i: (0, 0)),
            pl.BlockSpec((n_mem, 2 * xa_w), lambda i: (i // tiles_per_seq, 0)),
            pl.BlockSpec((xa_w, d), lambda i: (0, 0)),
        ],
        out_specs=pl.BlockSpec((tm, d), lambda i: (i, 0)),
        out_shape=jax.ShapeDtypeStruct((t, d), F32),
        compiler_params=_cparams(("parallel",)),
        name=name,
    )(x, g, wq, kv, wo)


def _router_body(x_ref, g_ref, wcat_ref, whi_ref, b_ref, tri_ref, hp_ref, route_ref, cnt_ref, base_ref):
    @pl.when(pl.program_id(0) == 0)
    def _():
        base_ref[...] = jnp.zeros_like(base_ref)

    h32 = _rms(x_ref[...], g_ref[...])
    hhi = h32.astype(BF16)
    hlo = (h32 - hhi.astype(F32)).astype(BF16)
    hp_ref[...] = _pack_bf16_pairs(h32)

    p = jnp.dot(hhi, wcat_ref[...], preferred_element_type=F32)
    lg = p[:, :ROUTE_LANES] + p[:, ROUTE_LANES:] + jnp.dot(hlo, whi_ref[...], preferred_element_type=F32)
    lg = lg + b_ref[...]

    tm = lg.shape[0]
    lane = lax.broadcasted_iota(I32, (tm, ROUTE_LANES), 1)
    lane_f = lane.astype(F32)
    neg = jnp.float32(-1e30)
    big = jnp.float32(ROUTE_LANES)

    def first_max(vals):
        m = jnp.max(vals, axis=-1, keepdims=True)
        idx = jnp.min(jnp.where(vals == m, lane_f, big), axis=-1, keepdims=True)
        return m, idx

    gmask = lane < N_GROUPS
    mg, g_sel = first_max(jnp.where(gmask, lg, neg))
    gate_g = 1.0 / jnp.sum(jnp.where(gmask, jnp.exp(lg - mg), 0.0), axis=-1, keepdims=True)

    grp_of_lane = ((lane - EXP_LANE0) >> 3).astype(F32)
    emask = (lane >= EXP_LANE0) & (lane < EXP_LANE0 + N_EXPERTS) & (grp_of_lane == g_sel)
    le = jnp.where(emask, lg, neg)
    m1, i1 = first_max(le)
    m2, i2 = first_max(jnp.where(lane_f == i1, neg, le))
    e21 = jnp.exp(m2 - m1)
    w1 = 1.0 / (1.0 + e21)
    w2 = e21 * w1
    g1 = gate_g * w1
    g2 = gate_g * w2

    is1 = lane_f == i1
    is2 = lane_f == i2
    onehot = (is1 | is2).astype(F32)
    before = jnp.dot(tri_ref[...], onehot.astype(BF16), preferred_element_type=F32) + base_ref[0:1, :]
    r1 = jnp.sum(jnp.where(is1, before, 0.0), axis=-1, keepdims=True)
    r2 = jnp.sum(jnp.where(is2, before, 0.0), axis=-1, keepdims=True)
    new_base = base_ref[...] + jnp.sum(onehot, axis=0, keepdims=True)
    base_ref[...] = new_base
    cnt_ref[...] = new_base

    e1 = i1 - float(EXP_LANE0)
    e2 = i2 - float(EXP_LANE0)
    slab = jnp.where(lane == 0, e1, jnp.where(lane == 1, e2, jnp.where(lane == 2, g1, jnp.where(
        lane == 3, g2, jnp.where(lane == 4, r1, jnp.where(lane == 5, r2, 0.0))))))
    route_ref[...] = slab


def _router(x, g, wcat, whi, bias, tri, *, tm, name):
    t, d = x.shape
    assert t % tm == 0
    return pl.pallas_call(
        _router_body,
        grid=(t // tm,),
        in_specs=[
            pl.BlockSpec((tm, d), lambda i: (i, 0)),
            pl.BlockSpec((1, d), lambda i: (0, 0)),
            pl.BlockSpec((d, 2 * ROUTE_LANES), lambda i: (0, 0)),
            pl.BlockSpec((d, ROUTE_LANES), lambda i: (0, 0)),
            pl.BlockSpec((1, ROUTE_LANES), lambda i: (0, 0)),
            pl.BlockSpec((tm, tm), lambda i: (0, 0)),
        ],
        out_specs=[
            pl.BlockSpec((tm, d // 2), lambda i: (i, 0)),
            pl.BlockSpec((tm, ROUTE_LANES), lambda i: (i, 0)),
            pl.BlockSpec((SUBLANES, ROUTE_LANES), lambda i: (0, 0)),
        ],
        out_shape=[
            jax.ShapeDtypeStruct((t, d // 2), U32),
            jax.ShapeDtypeStruct((t, ROUTE_LANES), F32),
            jax.ShapeDtypeStruct((SUBLANES, ROUTE_LANES), F32),
        ],
        scratch_shapes=[pltpu.VMEM((SUBLANES, ROUTE_LANES), F32)],
        compiler_params=_cparams(("arbitrary",)),
        name=name,
    )(x, g, wcat, whi, bias, tri)


def _row_copies_wait(src_ref, dst_ref, sem):
    pltpu.make_async_copy(src_ref, dst_ref, sem).wait()


def _dispatch_body(hp_ref, dest_hbm, xs_in_hbm, xs_hbm, idx_smem, sem_idx, sem_rows, *, tm):
    del xs_in_hbm
    i = pl.program_id(0)
    idx_cp = pltpu.make_async_copy(dest_hbm.at[pl.ds(i * (2 * tm), 2 * tm)], idx_smem, sem_idx)
    idx_cp.start()
    idx_cp.wait()

    def issue(t, carry):
        for k in range(2):
            d = idx_smem[2 * t + k]
            pltpu.make_async_copy(hp_ref.at[pl.ds(t, 1)], xs_hbm.at[pl.ds(d, 1)], sem_rows).start()
        return carry

    lax.fori_loop(0, tm, issue, 0, unroll=8)
    for k in range(2):
        _row_copies_wait(hp_ref, xs_hbm.at[pl.ds(0, tm)], sem_rows)


def _dispatch(hp, dest, xs_init, *, tm, name):
    t, dw = hp.shape
    assert t % tm == 0
    return pl.pallas_call(
        functools.partial(_dispatch_body, tm=tm),
        grid=(t // tm,),
        in_specs=[
            pl.BlockSpec((tm, dw), lambda i: (i, 0)),
            pl.BlockSpec(memory_space=pl.ANY),
            pl.BlockSpec(memory_space=pl.ANY),
        ],
        out_specs=pl.BlockSpec(memory_space=pl.ANY),
        out_shape=jax.ShapeDtypeStruct(xs_init.shape, xs_init.dtype),
        scratch_shapes=[
            pltpu.SMEM((2 * tm,), I32),
            pltpu.SemaphoreType.DMA,
            pltpu.SemaphoreType.DMA,
        ],
        input_output_aliases={2: 0},
        compiler_params=_cparams(("arbitrary",)),
        name=name,
    )(hp, dest, xs_init)


def _expert_body(be_ref, nb_ref, xs_ref, wu_ref, wd_ref, ys_ref):
    del be_ref
    used = pl.program_id(0) < nb_ref[0]

    @pl.when(jnp.logical_not(used))
    def _():
        ys_ref[...] = jnp.zeros_like(ys_ref)

    @pl.when(used)
    def _():
        w = xs_ref[...]
        half = w.shape[1]
        hidden = wd_ref.shape[1]
        hu = jnp.dot(_unpack_lo(w).astype(BF16), wu_ref[0, :half, :], preferred_element_type=F32)
        hu = hu + jnp.dot(_unpack_hi(w).astype(BF16), wu_ref[0, half:, :], preferred_element_type=F32)
        gate = hu[:, :hidden]
        hh = (gate * jax.nn.sigmoid(gate) * hu[:, hidden:]).astype(BF16)
        out = jnp.dot(hh, wd_ref[0], preferred_element_type=F32)
        ys_ref[...] = _pack_bf16_pairs(out)


def _experts(block_e, n_used, xs, w_up, w_down, *, name):
    n_rows, dw = xs.shape
    n_blocks = n_rows // MOE_BM
    _, d, two_f = w_up.shape
    hidden = w_down.shape[1]

    def row_map(i, be, nb):
        return (jnp.minimum(i, nb[0] - 1), 0)

    def w_map(i, be, nb):
        return (be[i], 0, 0)

    return pl.pallas_call(
        _expert_body,
        grid_spec=pltpu.PrefetchScalarGridSpec(
            num_scalar_prefetch=2,
            grid=(n_blocks,),
            in_specs=[
                pl.BlockSpec((MOE_BM, dw), row_map),
                pl.BlockSpec((1, d, two_f), w_map),
                pl.BlockSpec((1, hidden, d), w_map),
            ],
            out_specs=pl.BlockSpec((MOE_BM, dw), lambda i, be, nb: (i, 0)),
        ),
        out_shape=jax.ShapeDtypeStruct((n_rows, dw), U32),
        compiler_params=_cparams(("arbitrary",)),
        name=name,
    )(block_e, n_used, xs, w_up, w_down)


def _combine_body(x_ref, route_ref, dest_hbm, ys_hbm, o_ref, idx_smem, rows_ref, sem_idx, sem_rows, *, tm):
    i = pl.program_id(0)
    idx_cp = pltpu.make_async_copy(dest_hbm.at[pl.ds(i * (2 * tm), 2 * tm)], idx_smem, sem_idx)
    idx_cp.start()
    idx_cp.wait()

    def issue(t, carry):
        for k in range(2):
            d = idx_smem[2 * t + k]
            pltpu.make_async_copy(ys_hbm.at[pl.ds(d, 1)], rows_ref.at[k, pl.ds(t, 1)], sem_rows).start()
        return carry

    lax.fori_loop(0, tm, issue, 0, unroll=8)
    for k in range(2):
        _row_copies_wait(ys_hbm.at[pl.ds(0, tm)], rows_ref.at[k], sem_rows)

    half = rows_ref.shape[2]
    g0 = route_ref[:, 2:3]
    g1 = route_ref[:, 3:4]
    w0 = rows_ref[0]
    w1 = rows_ref[1]
    o_ref[:, :half] = x_ref[:, :half] + (g0 * _unpack_lo(w0) + g1 * _unpack_lo(w1))
    o_ref[:, half:] = x_ref[:, half:] + (g0 * _unpack_hi(w0) + g1 * _unpack_hi(w1))


def _combine(x, route, dest, ys, *, tm, name):
    t, d = x.shape
    dw = ys.shape[1]
    assert t % tm == 0
    return pl.pallas_call(
        functools.partial(_combine_body, tm=tm),
        grid=(t // tm,),
        in_specs=[
            pl.BlockSpec((tm, d), lambda i: (i, 0)),
            pl.BlockSpec((tm, ROUTE_LANES), lambda i: (i, 0)),
            pl.BlockSpec(memory_space=pl.ANY),
            pl.BlockSpec(memory_space=pl.ANY),
        ],
        out_specs=pl.BlockSpec((tm, d), lambda i: (i, 0)),
        out_shape=jax.ShapeDtypeStruct((t, d), F32),
        scratch_shapes=[
            pltpu.SMEM((2 * tm,), I32),
            pltpu.VMEM((2, tm, dw), U32),
            pltpu.SemaphoreType.DMA,
            pltpu.SemaphoreType.DMA,
        ],
        compiler_params=_cparams(("arbitrary",)),
        name=name,
    )(x, route, dest, ys)


def _final_norm_body(x_ref, g_ref, o_ref):
    o_ref[...] = _rms(x_ref[...], g_ref[...])


def _final_norm(x, g, *, row0, n_rows, tm, name):
    d = x.shape[1]
    tm = min(tm, n_rows)
    assert n_rows % tm == 0 and row0 % tm == 0
    blk0 = row0 // tm
    return pl.pallas_call(
        _final_norm_body,
        grid=(n_rows // tm,),
        in_specs=[pl.BlockSpec((tm, d), lambda i: (blk0 + i, 0)), pl.BlockSpec((1, d), lambda i: (0, 0))],
        out_specs=pl.BlockSpec((tm, d), lambda i: (i, 0)),
        out_shape=jax.ShapeDtypeStruct((n_rows, d), F32),
        compiler_params=_cparams(("parallel",)),
        name=name,
    )(x, g)


def _moe(x, g, wg, bg, we, be, w_up, w_down, *, tag):
    t, d = x.shape
    tm_route = min(512, t)
    tm_rows = min(256, t)

    wr = jnp.zeros((d, ROUTE_LANES), F32).at[:, :N_GROUPS].set(wg).at[:, EXP_LANE0:EXP_LANE0 + N_EXPERTS].set(we)
    wr_hi = wr.astype(BF16)
    wr_lo = (wr - wr_hi.astype(F32)).astype(BF16)
    bias = jnp.zeros((1, ROUTE_LANES), F32).at[0, :N_GROUPS].set(bg).at[0, EXP_LANE0:EXP_LANE0 + N_EXPERTS].set(be)
    ii = lax.broadcasted_iota(I32, (tm_route, tm_route), 0)
    jj = lax.broadcasted_iota(I32, (tm_route, tm_route), 1)
    tri = (jj < ii).astype(BF16)

    hp, route, cnt = _router(x, g, jnp.concatenate([wr_hi, wr_lo], axis=1), wr_hi, bias, tri,
                             tm=tm_route, name=f"router_{tag}")

    counts = cnt[0, EXP_LANE0:EXP_LANE0 + N_EXPERTS].astype(I32)
    padded = ((counts + MOE_BM - 1) // MOE_BM) * MOE_BM
    pad_end = jnp.cumsum(padded)
    pad_start = pad_end - padded
    e_ids = route[:, 0:2].astype(I32)
    rank = route[:, 4:6].astype(I32)
    dest = (pad_start[e_ids] + rank).reshape(-1)
    n_blocks = (2 * t) // MOE_BM + N_EXPERTS
    n_used = (pad_end[-1] // MOE_BM).astype(I32)
    blk = jnp.arange(n_blocks, dtype=I32)
    block_e = jnp.clip(jnp.searchsorted(pad_end, blk * MOE_BM, side="right"), 0, N_EXPERTS - 1).astype(I32)
    block_e = jnp.where(blk < n_used, block_e, block_e[n_used - 1])

    xs = _dispatch(hp, dest, jnp.zeros((n_blocks * MOE_BM, d // 2), U32), tm=tm_rows, name=f"dispatch_{tag}")
    ys = _experts(block_e, n_used.reshape(1), xs, w_up, w_down, name=f"experts_{tag}")
    return _combine(x, route, dest, ys, tm=tm_rows, name=f"combine_{tag}")


def kernel(x_prompt, x_sample, mem_prompt, mem_sample, g_mix, w_in, w_out, conv_w, conv_b, lru_wa, lru_ba, lru_wx, lru_bx, lru_lambda, sg_norm, sg_w, sg_b, g_xattn, g_mem, xa_wq, xa_wkv, xa_wo, g_moe, router_wg, router_bg, router_we, router_be, moe_w_up, moe_w_down, g_final):
    bp, seq, d = x_prompt.shape
    bs = x_sample.shape[0]
    assert x_sample.shape[1] == seq
    n_batch = bp + bs
    t = n_batch * seq
    n_mem = mem_prompt.shape[1]
    depth = w_in.shape[0]
    width = w_out.shape[1]
    n_heads = width // HEAD_W

    x = jnp.concatenate([x_prompt, x_sample], axis=0).reshape(t, d)
    mem = jnp.concatenate([mem_prompt, mem_sample], axis=0).reshape(n_batch * n_mem, d)

    for i in range(depth):
        j = i // 2
        w_in_i = w_in[i].astype(BF16)
        w_out_i = w_out[i].astype(BF16)
        if i % 2 == 0:
            z = _norm_matmul(x, g_mix[i][None], w_in_i, tm=512, tn=1024, gelu_cols=width, out_dtype=BF16,
                             name=f"mix_in_{i}")
            wcat = jnp.concatenate([lru_wa[j, 0], lru_wx[j, 0], lru_wa[j, 1], lru_wx[j, 1]], axis=-1).astype(BF16)
            bcat = jnp.concatenate([lru_ba[j, 0].reshape(n_heads, 1, HEAD_W), lru_bx[j, 0].reshape(n_heads, 1, HEAD_W),
                                    lru_ba[j, 1].reshape(n_heads, 1, HEAD_W), lru_bx[j, 1].reshape(n_heads, 1, HEAD_W)],
                                   axis=-1)
            y = _lru_core(z, conv_w[j], conv_b[j][None], wcat, bcat, lru_lambda[j], n_batch=n_batch, seq=seq,
                          name=f"lru_{i}")
            x = _res_matmul(y, w_out_i, x, tm=512, tn=1024, name=f"mix_out_{i}")
        else:
            z = _norm_matmul(x, g_mix[i][None], w_in_i, tm=512, tn=1024, gelu_cols=2 * width, out_dtype=BF16,
                             name=f"mix_in_{i}")
            x = _gmlp_out(z, x, sg_norm[j][None], sg_w[j].astype(BF16), sg_b[j].T, w_out_i, tm=256,
                          name=f"gmlp_{i}")

        kv = _norm_matmul(mem, g_mem[i][None], xa_wkv[i].astype(BF16), tm=n_mem, tn=1024, gelu_cols=0,
                          out_dtype=BF16, name=f"mem_kv_{i}")
        x = _xattn(x, g_xattn[i][None], xa_wq[i].astype(BF16), kv, xa_wo[i].astype(BF16), seq=seq, n_mem=n_mem,
                   tm=512, name=f"xattn_{i}")
        x = _moe(x, g_moe[i][None], router_wg[i], router_bg[i], router_we[i], router_be[i],
                 moe_w_up[i].astype(BF16), moe_w_down[i].astype(BF16), tag=str(i))

    y_prompt = _final_norm(x, g_final[None], row0=0, n_rows=bp * seq, tm=512, name="final_prompt")
    y_sample = _final_norm(x, g_final[None], row0=bp * seq, n_rows=bs * seq, tm=512, name="final_sample")
    return (y_prompt.reshape(bp, seq, d), y_sample.reshape(bs, seq, d))
```

```python
import functools

import jax
import jax.numpy as jnp
from jax import lax
from jax.experimental import pallas as pl
from jax.experimental.pallas import tpu as pltpu

F32 = jnp.float32
BF16 = jnp.bfloat16
U32 = jnp.uint32
I32 = jnp.int32

EPS = 1e-6
LRU_C = 8.0
LOG2_E = 1.4426950408889634
TINY = 1e-30
HEAD_W = 128
CHUNK = 128
CONV_W = 4
XA_HEADS = 4
XA_HEAD_DIM = 128
N_GROUPS = 8
EXPERTS_PER_GROUP = 8
N_EXPERTS = 64
ROUTE_LANES = 128
EXP_LANE0 = N_GROUPS

VMEM_LIMIT_BYTES = 56 * 1024 * 1024
SUBLANES = 8

MOE_BM = 256
HI_MASK = 0xFFFF0000


def _cparams(semantics):
    return pltpu.CompilerParams(dimension_semantics=semantics, vmem_limit_bytes=VMEM_LIMIT_BYTES)


def _gelu(x):
    return 0.5 * x * (1.0 + jnp.tanh(0.7978845608028654 * (x + 0.044715 * (x * x * x))))


def _sigmoid(x):
    return 0.5 * jnp.tanh(0.5 * x) + 0.5


def _rms(x, g):
    ms = jnp.mean(x * x, axis=-1, keepdims=True)
    return x * lax.rsqrt(ms + EPS) * g


def _pack_bf16_pairs(x):
    n = x.shape[1] // 2
    bits = lax.bitcast_convert_type(x.astype(BF16).astype(F32), U32)
    return (bits[:, :n] >> 16) | (bits[:, n:] & jnp.uint32(HI_MASK))


def _unpack_lo(w):
    return lax.bitcast_convert_type(w << 16, F32)


def _unpack_hi(w):
    return lax.bitcast_convert_type(w & jnp.uint32(HI_MASK), F32)


def _norm_matmul_body(x_ref, g_ref, w_ref, o_ref, *, tn, n_gelu):
    h = _rms(x_ref[...], g_ref[...]).astype(BF16)
    for jc in range(w_ref.shape[1] // tn):
        cs = slice(jc * tn, (jc + 1) * tn)
        z = jnp.dot(h, w_ref[:, cs], preferred_element_type=F32)
        o_ref[:, cs] = (_gelu(z) if jc < n_gelu else z).astype(o_ref.dtype)


def _norm_matmul(x, g, w, *, tm, tn, gelu_cols, out_dtype, name):
    t, d = x.shape
    n = w.shape[1]
    tm = min(tm, t)
    tn = min(tn, n)
    assert t % tm == 0 and n % tn == 0 and gelu_cols % tn == 0
    return pl.pallas_call(
        functools.partial(_norm_matmul_body, tn=tn, n_gelu=gelu_cols // tn),
        grid=(t // tm,),
        in_specs=[
            pl.BlockSpec((tm, d), lambda i: (i, 0)),
            pl.BlockSpec((1, d), lambda i: (0, 0)),
            pl.BlockSpec((d, n), lambda i: (0, 0)),
        ],
        out_specs=pl.BlockSpec((tm, n), lambda i: (i, 0)),
        out_shape=jax.ShapeDtypeStruct((t, n), out_dtype),
        compiler_params=_cparams(("parallel",)),
        name=name,
    )(x, g, w)


def _res_matmul_body(y_ref, w_ref, x_ref, o_ref, *, tn):
    y = y_ref[...]
    for jc in range(w_ref.shape[1] // tn):
        cs = slice(jc * tn, (jc + 1) * tn)
        o_ref[:, cs] = x_ref[:, cs] + jnp.dot(y, w_ref[:, cs], preferred_element_type=F32)


def _res_matmul(y, w, x, *, tm, tn, name):
    t, k = y.shape
    n = w.shape[1]
    tm = min(tm, t)
    assert t % tm == 0 and n % tn == 0
    return pl.pallas_call(
        functools.partial(_res_matmul_body, tn=tn),
        grid=(t // tm,),
        in_specs=[
            pl.BlockSpec((tm, k), lambda i: (i, 0)),
            pl.BlockSpec((k, n), lambda i: (0, 0)),
            pl.BlockSpec((tm, n), lambda i: (i, 0)),
        ],
        out_specs=pl.BlockSpec((tm, n), lambda i: (i, 0)),
        out_shape=jax.ShapeDtypeStruct((t, n), F32),
        compiler_params=_cparams(("parallel",)),
        name=name,
    )(y, w, x)


LRU_CW = 256
LRU_ROWS = 256
LRU_PAD = 8


def _lru_body(zg_ref, zx_ref, cw_ref, cb_ref, w_ref, b_ref, lam_ref, o_ref,
              xf_ref, af_ref, bf_ref, ab_ref, bb_ref, *, seq):
    cw = LRU_CW
    rows = LRU_ROWS
    zero_pad = jnp.zeros((LRU_PAD, cw), F32)
    xf_ref[0:LRU_PAD, :] = zero_pad
    xf_ref[LRU_PAD + seq:LRU_PAD + seq + LRU_PAD, :] = zero_pad
    xf_ref[LRU_PAD:LRU_PAD + seq, :] = zx_ref[...].astype(F32)

    nl = -lam_ref[...]
    softplus = jnp.maximum(nl, 0.0) + jnp.log1p(jnp.exp(-jnp.abs(nl)))
    decay = (-LRU_C * LOG2_E) * softplus
    cwv = cw_ref[...]
    cbv = cb_ref[...]
    dirs = ((af_ref, bf_ref), (ab_ref, bb_ref))

    def gate_chunk(c, carry):
        t0 = pl.multiple_of(c * rows, rows)
        xw = xf_ref[pl.ds(t0, rows + 2 * LRU_PAD), :]
        xc = cbv
        for k in range(CONV_W):
            off = LRU_PAD + k - CONV_W // 2
            xc = xc + xw[off:off + rows, :] * cwv[k:k + 1, :]
        for hh in range(cw // HEAD_W):
            sl = slice(hh * HEAD_W, (hh + 1) * HEAD_W)
            xch = xc[:, sl]
            gates = jnp.dot(xch.astype(BF16), w_ref[hh], preferred_element_type=F32) + b_ref[hh]
            for d, (a_ref, b_ref_d) in enumerate(dirs):
                r = _sigmoid(gates[:, (2 * d) * HEAD_W:(2 * d + 1) * HEAD_W])
                i = _sigmoid(gates[:, (2 * d + 1) * HEAD_W:(2 * d + 2) * HEAD_W])
                a = jnp.exp2(r * decay[d:d + 1, sl])
                s = 1.0 - a * a
                a_ref[pl.ds(t0, rows), sl] = a
                b_ref_d[pl.ds(t0, rows), sl] = (s * lax.rsqrt(jnp.maximum(s, TINY))) * (i * xch)
        return carry

    lax.fori_loop(0, seq // rows, gate_chunk, 0)

    row = lax.broadcasted_iota(I32, (SUBLANES, cw), 0)
    n_tiles = seq // SUBLANES

    def scan_tile(k, carry):
        cf, cb = carry
        tf = pl.multiple_of(k * SUBLANES, SUBLANES)
        a = af_ref[pl.ds(tf, SUBLANES), :]
        b = bf_ref[pl.ds(tf, SUBLANES), :]
        for s in (1, 2, 4):
            m = row >= s
            b = jnp.where(m, a * pltpu.roll(b, s, 0) + b, b)
            a = jnp.where(m, a * pltpu.roll(a, s, 0), a)
        h = a * cf + b
        bf_ref[pl.ds(tf, SUBLANES), :] = h
        cf = jnp.broadcast_to(h[SUBLANES - 1:SUBLANES, :], (SUBLANES, cw))
        tb = pl.multiple_of((n_tiles - 1 - k) * SUBLANES, SUBLANES)
        a = ab_ref[pl.ds(tb, SUBLANES), :]
        b = bb_ref[pl.ds(tb, SUBLANES), :]
        for s in (1, 2, 4):
            m = row < SUBLANES - s
            b = jnp.where(m, a * pltpu.roll(b, SUBLANES - s, 0) + b, b)
            a = jnp.where(m, a * pltpu.roll(a, SUBLANES - s, 0), a)
        h = a * cb + b
        bb_ref[pl.ds(tb, SUBLANES), :] = h
        cb = jnp.broadcast_to(h[0:1, :], (SUBLANES, cw))
        return cf, cb

    zero_tile = jnp.zeros((SUBLANES, cw), F32)
    lax.fori_loop(0, n_tiles, scan_tile, (zero_tile, zero_tile), unroll=4)

    def out_chunk(c, carry):
        t0 = pl.multiple_of(c * rows, rows)
        y = zg_ref[pl.ds(t0, rows), :].astype(F32) * (bf_ref[pl.ds(t0, rows), :] + bb_ref[pl.ds(t0, rows), :])
        o_ref[pl.ds(t0, rows), :] = y.astype(BF16)
        return carry

    lax.fori_loop(0, seq // rows, out_chunk, 0)


def _lru_core(z, conv_w, conv_b, wcat, bcat, lam, *, n_batch, seq, name):
    t, two_w = z.shape
    width = two_w // 2
    cw = LRU_CW
    n_ct = width // cw
    hpt = cw // HEAD_W
    assert seq % LRU_ROWS == 0 and width % cw == 0
    return pl.pallas_call(
        functools.partial(_lru_body, seq=seq),
        grid=(n_batch, n_ct),
        in_specs=[
            pl.BlockSpec((seq, cw), lambda b, c: (b, c)),
            pl.BlockSpec((seq, cw), lambda b, c: (b, n_ct + c)),
            pl.BlockSpec((CONV_W, cw), lambda b, c: (0, c)),
            pl.BlockSpec((1, cw), lambda b, c: (0, c)),
            pl.BlockSpec((hpt, HEAD_W, 4 * HEAD_W), lambda b, c: (c, 0, 0)),
            pl.BlockSpec((hpt, 1, 4 * HEAD_W), lambda b, c: (c, 0, 0)),
            pl.BlockSpec((2, cw), lambda b, c: (0, c)),
        ],
        out_specs=pl.BlockSpec((seq, cw), lambda b, c: (b, c)),
        out_shape=jax.ShapeDtypeStruct((t, width), BF16),
        scratch_shapes=[
            pltpu.VMEM((seq + 2 * LRU_PAD, cw), F32),
            pltpu.VMEM((seq, cw), F32),
            pltpu.VMEM((seq, cw), F32),
            pltpu.VMEM((seq, cw), F32),
            pltpu.VMEM((seq, cw), F32),
        ],
        compiler_params=_cparams(("parallel", "parallel")),
        name=name,
    )(z, z, conv_w, conv_b, wcat, bcat, lam)


def _gmlp_body(z_ref, x_ref, sgn_ref, sgw_ref, sgbt_ref, w_ref, o_ref, vn_ref, y_ref, *, tm, width):
    v = z_ref[:, width:].astype(F32)
    vn_ref[...] = _rms(v, sgn_ref[...]).astype(BF16)
    for n in range(tm // CHUNK):
        rs = slice(n * CHUNK, (n + 1) * CHUNK)
        for h in range(width // HEAD_W):
            cs = slice(h * HEAD_W, (h + 1) * HEAD_W)
            mixed = jnp.dot(sgw_ref[h], vn_ref[rs, cs], preferred_element_type=F32) + sgbt_ref[:, h:h + 1]
            y_ref[rs, cs] = (z_ref[rs, cs].astype(F32) * mixed).astype(BF16)
    o_ref[...] = x_ref[...] + jnp.dot(y_ref[...], w_ref[...], preferred_element_type=F32)


def _gmlp_out(z, x, sg_norm, sg_w, sg_bt, w_out, *, tm, name):
    t, two_w = z.shape
    width = two_w // 2
    d = w_out.shape[1]
    n_heads = width // HEAD_W
    tm = min(tm, t)
    assert t % tm == 0 and tm % CHUNK == 0
    return pl.pallas_call(
        functools.partial(_gmlp_body, tm=tm, width=width),
        grid=(t // tm,),
        in_specs=[
            pl.BlockSpec((tm, two_w), lambda i: (i, 0)),
            pl.BlockSpec((tm, d), lambda i: (i, 0)),
            pl.BlockSpec((1, width), lambda i: (0, 0)),
            pl.BlockSpec((n_heads, CHUNK, CHUNK), lambda i: (0, 0, 0)),
            pl.BlockSpec((CHUNK, n_heads), lambda i: (0, 0)),
            pl.BlockSpec((width, d), lambda i: (0, 0)),
        ],
        out_specs=pl.BlockSpec((tm, d), lambda i: (i, 0)),
        out_shape=jax.ShapeDtypeStruct((t, d), F32),
        scratch_shapes=[pltpu.VMEM((tm, width), BF16), pltpu.VMEM((tm, width), BF16)],
        compiler_params=_cparams(("parallel",)),
        name=name,
    )(z, x, sg_norm, sg_w, sg_bt, w_out)


def _xattn_body(x_ref, g_ref, wq_ref, kv_ref, wo_ref, o_ref):
    x = x_ref[...]
    h = _rms(x, g_ref[...]).astype(BF16)
    q = jnp.dot(h, wq_ref[...], preferred_element_type=F32) * (XA_HEAD_DIM ** -0.5)
    qb = q.astype(BF16)
    xa_w = XA_HEADS * XA_HEAD_DIM
    outs = []
    for hd in range(XA_HEADS):
        cs = slice(hd * XA_HEAD_DIM, (hd + 1) * XA_HEAD_DIM)
        vs = slice(xa_w + hd * XA_HEAD_DIM, xa_w + (hd + 1) * XA_HEAD_DIM)
        s = lax.dot_general(qb[:, cs], kv_ref[:, cs], (((1,), (1,)), ((), ())), preferred_element_type=F32)
        p = jnp.exp(s - jnp.max(s, axis=-1, keepdims=True))
        l = jnp.sum(p, axis=-1, keepdims=True)
        o = jnp.dot(p.astype(BF16), kv_ref[:, vs], preferred_element_type=F32) / l
        outs.append(o.astype(BF16))
    o = jnp.concatenate(outs, axis=1)
    o_ref[...] = x + jnp.dot(o, wo_ref[...], preferred_element_type=F32)


def _xattn(x, g, wq, kv, wo, *, seq, n_mem, tm, name):
    t, d = x.shape
    xa_w = wq.shape[1]
    tm = min(tm, seq)
    assert seq % tm == 0
    tiles_per_seq = seq // tm
    return pl.pallas_call(
        _xattn_body,
        grid=(t // tm,),
        in_specs=[
            pl.BlockSpec((tm, d), lambda i: (i, 0)),
            pl.BlockSpec((1, d), lambda i: (0, 0)),
            pl.BlockSpec((d, xa_w), lambda i: (0, 0)),
            pl.BlockSpec((n_mem, 2 * xa_w), lambda i: (i // tiles_per_seq, 0)),
            pl.BlockSpec((xa_w, d), lambda i: (0, 0)),
        ],
        out_specs=pl.BlockSpec((tm, d), lambda i: (i, 0)),
        out_shape=jax.ShapeDtypeStruct((t, d), F32),
        compiler_params=_cparams(("parallel",)),
        name=name,
    )(x, g, wq, kv, wo)


def _router_body(x_ref, g_ref, wcat_ref, whi_ref, b_ref, tri_ref, hp_ref, route_ref, cnt_ref, base_ref):
    @pl.when(pl.program_id(0) == 0)
    def _():
        base_ref[...] = jnp.zeros_like(base_ref)

    h32 = _rms(x_ref[...], g_ref[...])
    hhi = h32.astype(BF16)
    hlo = (h32 - hhi.astype(F32)).astype(BF16)
    hp_ref[...] = _pack_bf16_pairs(h32)

    p = jnp.dot(hhi, wcat_ref[...], preferred_element_type=F32)
    lg = p[:, :ROUTE_LANES] + p[:, ROUTE_LANES:] + jnp.dot(hlo, whi_ref[...], preferred_element_type=F32)
    lg = lg + b_ref[...]

    tm = lg.shape[0]
    lane = lax.broadcasted_iota(I32, (tm, ROUTE_LANES), 1)
    lane_f = lane.astype(F32)
    neg = jnp.float32(-1e30)
    big = jnp.float32(ROUTE_LANES)

    def first_max(vals):
        m = jnp.max(vals, axis=-1, keepdims=True)
        idx = jnp.min(jnp.where(vals == m, lane_f, big), axis=-1, keepdims=True)
        return m, idx

    gmask = lane < N_GROUPS
    mg, g_sel = first_max(jnp.where(gmask, lg, neg))
    gate_g = 1.0 / jnp.sum(jnp.where(gmask, jnp.exp(lg - mg), 0.0), axis=-1, keepdims=True)

    grp_of_lane = ((lane - EXP_LANE0) >> 3).astype(F32)
    emask = (lane >= EXP_LANE0) & (lane < EXP_LANE0 + N_EXPERTS) & (grp_of_lane == g_sel)
    le = jnp.where(emask, lg, neg)
    m1, i1 = first_max(le)
    m2, i2 = first_max(jnp.where(lane_f == i1, neg, le))
    e21 = jnp.exp(m2 - m1)
    w1 = 1.0 / (1.0 + e21)
    w2 = e21 * w1
    g1 = gate_g * w1
    g2 = gate_g * w2

    is1 = lane_f == i1
    is2 = lane_f == i2
    onehot = (is1 | is2).astype(F32)
    before = jnp.dot(tri_ref[...], onehot.astype(BF16), preferred_element_type=F32) + base_ref[0:1, :]
    r1 = jnp.sum(jnp.where(is1, before, 0.0), axis=-1, keepdims=True)
    r2 = jnp.sum(jnp.where(is2, before, 0.0), axis=-1, keepdims=True)
    new_base = base_ref[...] + jnp.sum(onehot, axis=0, keepdims=True)
    base_ref[...] = new_base
    cnt_ref[...] = new_base

    e1 = i1 - float(EXP_LANE0)
    e2 = i2 - float(EXP_LANE0)
    slab = jnp.where(lane == 0, e1, jnp.where(lane == 1, e2, jnp.where(lane == 2, g1, jnp.where(
        lane == 3, g2, jnp.where(lane == 4, r1, jnp.where(lane == 5, r2, 0.0))))))
    route_ref[...] = slab


def _router(x, g, wcat, whi, bias, tri, *, tm, name):
    t, d = x.shape
    assert t % tm == 0
    return pl.pallas_call(
        _router_body,
        grid=(t // tm,),
        in_specs=[
            pl.BlockSpec((tm, d), lambda i: (i, 0)),
            pl.BlockSpec((1, d), lambda i: (0, 0)),
            pl.BlockSpec((d, 2 * ROUTE_LANES), lambda i: (0, 0)),
            pl.BlockSpec((d, ROUTE_LANES), lambda i: (0, 0)),
            pl.BlockSpec((1, ROUTE_LANES), lambda i: (0, 0)),
            pl.BlockSpec((tm, tm), lambda i: (0, 0)),
        ],
        out_specs=[
            pl.BlockSpec((tm, d // 2), lambda i: (i, 0)),
            pl.BlockSpec((tm, ROUTE_LANES), lambda i: (i, 0)),
            pl.BlockSpec((SUBLANES, ROUTE_LANES), lambda i: (0, 0)),
        ],
        out_shape=[
            jax.ShapeDtypeStruct((t, d // 2), U32),
            jax.ShapeDtypeStruct((t, ROUTE_LANES), F32),
            jax.ShapeDtypeStruct((SUBLANES, ROUTE_LANES), F32),
        ],
        scratch_shapes=[pltpu.VMEM((SUBLANES, ROUTE_LANES), F32)],
        compiler_params=_cparams(("arbitrary",)),
        name=name,
    )(x, g, wcat, whi, bias, tri)


def _row_copies_wait(src_ref, dst_ref, sem):
    pltpu.make_async_copy(src_ref, dst_ref, sem).wait()


def _dispatch_body(hp_ref, dest_hbm, xs_in_hbm, xs_hbm, idx_smem, sem_idx, sem_rows, *, tm):
    del xs_in_hbm
    i = pl.program_id(0)
    idx_cp = pltpu.make_async_copy(dest_hbm.at[pl.ds(i * (2 * tm), 2 * tm)], idx_smem, sem_idx)
    idx_cp.start()
    idx_cp.wait()

    def issue(t, carry):
        for k in range(2):
            d = idx_smem[2 * t + k]
            pltpu.make_async_copy(hp_ref.at[pl.ds(t, 1)], xs_hbm.at[pl.ds(d, 1)], sem_rows).start(priority=k)
        return carry

    lax.fori_loop(0, tm, issue, 0, unroll=8)
    for k in range(2):
        _row_copies_wait(hp_ref, xs_hbm.at[pl.ds(0, tm)], sem_rows)


def _dispatch(hp, dest, xs_init, *, tm, name):
    t, dw = hp.shape
    assert t % tm == 0
    return pl.pallas_call(
        functools.partial(_dispatch_body, tm=tm),
        grid=(t // tm,),
        in_specs=[
            pl.BlockSpec((tm, dw), lambda i: (i, 0)),
            pl.BlockSpec(memory_space=pl.ANY),
            pl.BlockSpec(memory_space=pl.ANY),
        ],
        out_specs=pl.BlockSpec(memory_space=pl.ANY),
        out_shape=jax.ShapeDtypeStruct(xs_init.shape, xs_init.dtype),
        scratch_shapes=[
            pltpu.SMEM((2 * tm,), I32),
            pltpu.SemaphoreType.DMA,
            pltpu.SemaphoreType.DMA,
        ],
        input_output_aliases={2: 0},
        compiler_params=_cparams(("arbitrary",)),
        name=name,
    )(hp, dest, xs_init)


CAST_ROWS = 256


def _stage_expert_weight(be_ref, nb_ref, w_ref, wb_ref):
    i = pl.program_id(0)
    fresh = jnp.logical_or(i == 0, be_ref[i] != be_ref[jnp.maximum(i - 1, 0)])

    @pl.when(jnp.logical_and(i < nb_ref[0], fresh))
    def _():
        def cast_rows(c, carry):
            r0 = pl.multiple_of(c * CAST_ROWS, CAST_ROWS)
            wb_ref[pl.ds(r0, CAST_ROWS), :] = w_ref[0, 0, pl.ds(r0, CAST_ROWS), :].astype(BF16)
            return carry

        lax.fori_loop(0, wb_ref.shape[0] // CAST_ROWS, cast_rows, 0)


def _expert_up_body(be_ref, nb_ref, xs_ref, wu_ref, hh_ref, wub_ref):
    _stage_expert_weight(be_ref, nb_ref, wu_ref, wub_ref)
    used = pl.program_id(0) < nb_ref[0]

    @pl.when(jnp.logical_not(used))
    def _():
        hh_ref[...] = jnp.zeros_like(hh_ref)

    @pl.when(used)
    def _():
        w = xs_ref[...]
        half = w.shape[1]
        hidden = hh_ref.shape[1]
        hu = jnp.dot(_unpack_lo(w).astype(BF16), wub_ref[:half, :], preferred_element_type=F32)
        hu = hu + jnp.dot(_unpack_hi(w).astype(BF16), wub_ref[half:, :], preferred_element_type=F32)
        gate = hu[:, :hidden]
        hh_ref[...] = (gate * jax.nn.sigmoid(gate) * hu[:, hidden:]).astype(BF16)


def _expert_down_body(be_ref, nb_ref, hh_ref, wd_ref, ys_ref, wdb_ref):
    _stage_expert_weight(be_ref, nb_ref, wd_ref, wdb_ref)
    used = pl.program_id(0) < nb_ref[0]

    @pl.when(jnp.logical_not(used))
    def _():
        ys_ref[...] = jnp.zeros_like(ys_ref)

    @pl.when(used)
    def _():
        ys_ref[...] = _pack_bf16_pairs(jnp.dot(hh_ref[...], wdb_ref[...], preferred_element_type=F32))


def _experts(block_e, n_used, xs, w_up, w_down, *, layer, name):
    n_rows, dw = xs.shape
    n_blocks = n_rows // MOE_BM
    _, _, d, two_f = w_up.shape
    hidden = w_down.shape[2]

    def row_map(i, be, nb):
        return (jnp.minimum(i, nb[0] - 1), 0)

    def w_map(i, be, nb):
        return (layer, be[i], 0, 0)

    def out_map(i, be, nb):
        return (i, 0)

    hh = pl.pallas_call(
        _expert_up_body,
        grid_spec=pltpu.PrefetchScalarGridSpec(
            num_scalar_prefetch=2,
            grid=(n_blocks,),
            in_specs=[pl.BlockSpec((MOE_BM, dw), row_map), pl.BlockSpec((1, 1, d, two_f), w_map)],
            out_specs=pl.BlockSpec((MOE_BM, hidden), out_map),
            scratch_shapes=[pltpu.VMEM((d, two_f), BF16)],
        ),
        out_shape=jax.ShapeDtypeStruct((n_rows, hidden), BF16),
        compiler_params=_cparams(("arbitrary",)),
        name=f"{name}_up",
    )(block_e, n_used, xs, w_up)
    return pl.pallas_call(
        _expert_down_body,
        grid_spec=pltpu.PrefetchScalarGridSpec(
            num_scalar_prefetch=2,
            grid=(n_blocks,),
            in_specs=[pl.BlockSpec((MOE_BM, hidden), row_map), pl.BlockSpec((1, 1, hidden, d), w_map)],
            out_specs=pl.BlockSpec((MOE_BM, dw), out_map),
            scratch_shapes=[pltpu.VMEM((hidden, d), BF16)],
        ),
        out_shape=jax.ShapeDtypeStruct((n_rows, dw), U32),
        compiler_params=_cparams(("arbitrary",)),
        name=f"{name}_down",
    )(block_e, n_used, hh, w_down)


def _combine_body(x_ref, route_ref, dest_hbm, ys_hbm, o_ref, idx_smem, rows_ref, sem_idx, sem_rows, *, tm):
    i = pl.program_id(0)
    idx_cp = pltpu.make_async_copy(dest_hbm.at[pl.ds(i * (2 * tm), 2 * tm)], idx_smem, sem_idx)
    idx_cp.start()
    idx_cp.wait()

    def issue(t, carry):
        for k in range(2):
            d = idx_smem[2 * t + k]
            pltpu.make_async_copy(ys_hbm.at[pl.ds(d, 1)], rows_ref.at[k, pl.ds(t, 1)], sem_rows).start(priority=k)
        return carry

    lax.fori_loop(0, tm, issue, 0, unroll=8)
    for k in range(2):
        _row_copies_wait(ys_hbm.at[pl.ds(0, tm)], rows_ref.at[k], sem_rows)

    half = rows_ref.shape[2]
    g0 = route_ref[:, 2:3]
    g1 = route_ref[:, 3:4]
    w0 = rows_ref[0]
    w1 = rows_ref[1]
    o_ref[:, :half] = x_ref[:, :half] + (g0 * _unpack_lo(w0) + g1 * _unpack_lo(w1))
    o_ref[:, half:] = x_ref[:, half:] + (g0 * _unpack_hi(w0) + g1 * _unpack_hi(w1))


def _combine(x, route, dest, ys, *, tm, name):
    t, d = x.shape
    dw = ys.shape[1]
    assert t % tm == 0
    return pl.pallas_call(
        functools.partial(_combine_body, tm=tm),
        grid=(t // tm,),
        in_specs=[
            pl.BlockSpec((tm, d), lambda i: (i, 0)),
            pl.BlockSpec((tm, ROUTE_LANES), lambda i: (i, 0)),
            pl.BlockSpec(memory_space=pl.ANY),
            pl.BlockSpec(memory_space=pl.ANY),
        ],
        out_specs=pl.BlockSpec((tm, d), lambda i: (i, 0)),
        out_shape=jax.ShapeDtypeStruct((t, d), F32),
        scratch_shapes=[
            pltpu.SMEM((2 * tm,), I32),
            pltpu.VMEM((2, tm, dw), U32),
            pltpu.SemaphoreType.DMA,
            pltpu.SemaphoreType.DMA,
        ],
        compiler_params=_cparams(("arbitrary",)),
        name=name,
    )(x, route, dest, ys)


def _final_norm_body(x_ref, g_ref, o_ref):
    o_ref[...] = _rms(x_ref[...], g_ref[...])


def _final_norm(x, g, *, row0, n_rows, tm, name):
    d = x.shape[1]
    tm = min(tm, n_rows)
    assert n_rows % tm == 0 and row0 % tm == 0
    blk0 = row0 // tm
    return pl.pallas_call(
        _final_norm_body,
        grid=(n_rows // tm,),
        in_specs=[pl.BlockSpec((tm, d), lambda i: (blk0 + i, 0)), pl.BlockSpec((1, d), lambda i: (0, 0))],
        out_specs=pl.BlockSpec((tm, d), lambda i: (i, 0)),
        out_shape=jax.ShapeDtypeStruct((n_rows, d), F32),
        compiler_params=_cparams(("parallel",)),
        name=name,
    )(x, g)


def _moe(x, g, wg, bg, we, be, w_up, w_down, *, layer):
    tag = str(layer)
    t, d = x.shape
    tm_route = min(512, t)
    tm_rows = min(256, t)

    wr = jnp.zeros((d, ROUTE_LANES), F32).at[:, :N_GROUPS].set(wg).at[:, EXP_LANE0:EXP_LANE0 + N_EXPERTS].set(we)
    wr_hi = wr.astype(BF16)
    wr_lo = (wr - wr_hi.astype(F32)).astype(BF16)
    bias = jnp.zeros((1, ROUTE_LANES), F32).at[0, :N_GROUPS].set(bg).at[0, EXP_LANE0:EXP_LANE0 + N_EXPERTS].set(be)
    ii = lax.broadcasted_iota(I32, (tm_route, tm_route), 0)
    jj = lax.broadcasted_iota(I32, (tm_route, tm_route), 1)
    tri = (jj < ii).astype(BF16)

    hp, route, cnt = _router(x, g, jnp.concatenate([wr_hi, wr_lo], axis=1), wr_hi, bias, tri,
                             tm=tm_route, name=f"router_{tag}")

    counts = cnt[0, EXP_LANE0:EXP_LANE0 + N_EXPERTS].astype(I32)
    padded = ((counts + MOE_BM - 1) // MOE_BM) * MOE_BM
    pad_end = jnp.cumsum(padded)
    pad_start = pad_end - padded
    e_ids = route[:, 0:2].astype(I32)
    rank = route[:, 4:6].astype(I32)
    dest = (pad_start[e_ids] + rank).reshape(-1)
    n_blocks = (2 * t) // MOE_BM + N_EXPERTS
    n_used = (pad_end[-1] // MOE_BM).astype(I32)
    blk = jnp.arange(n_blocks, dtype=I32)
    block_e = jnp.sum((pad_end[None, :] <= (jnp.minimum(blk, n_used - 1) * MOE_BM)[:, None]).astype(I32), axis=1)

    xs = _dispatch(hp, dest, jnp.zeros((n_blocks * MOE_BM, d // 2), U32), tm=tm_rows, name=f"dispatch_{tag}")
    ys = _experts(block_e, n_used.reshape(1), xs, w_up, w_down, layer=layer, name=f"experts_{tag}")
    return _combine(x, route, dest, ys, tm=tm_rows, name=f"combine_{tag}")


def kernel(x_prompt, x_sample, mem_prompt, mem_sample, g_mix, w_in, w_out, conv_w, conv_b, lru_wa, lru_ba, lru_wx, lru_bx, lru_lambda, sg_norm, sg_w, sg_b, g_xattn, g_mem, xa_wq, xa_wkv, xa_wo, g_moe, router_wg, router_bg, router_we, router_be, moe_w_up, moe_w_down, g_final):
    bp, seq, d = x_prompt.shape
    bs = x_sample.shape[0]
    assert x_sample.shape[1] == seq
    n_batch = bp + bs
    t = n_batch * seq
    n_mem = mem_prompt.shape[1]
    depth = w_in.shape[0]
    width = w_out.shape[1]
    n_heads = width // HEAD_W

    x = jnp.concatenate([x_prompt, x_sample], axis=0).reshape(t, d)
    mem = jnp.concatenate([mem_prompt, mem_sample], axis=0).reshape(n_batch * n_mem, d)

    for i in range(depth):
        j = i // 2
        w_in_i = w_in[i].astype(BF16)
        w_out_i = w_out[i].astype(BF16)
        if i % 2 == 0:
            z = _norm_matmul(x, g_mix[i][None], w_in_i, tm=512, tn=1024, gelu_cols=width, out_dtype=BF16,
                             name=f"mix_in_{i}")
            wcat = jnp.concatenate([lru_wa[j, 0], lru_wx[j, 0], lru_wa[j, 1], lru_wx[j, 1]], axis=-1).astype(BF16)
            bcat = jnp.concatenate([lru_ba[j, 0].reshape(n_heads, 1, HEAD_W), lru_bx[j, 0].reshape(n_heads, 1, HEAD_W),
                                    lru_ba[j, 1].reshape(n_heads, 1, HEAD_W), lru_bx[j, 1].reshape(n_heads, 1, HEAD_W)],
                                   axis=-1)
            y = _lru_core(z, conv_w[j], conv_b[j][None], wcat, bcat, lru_lambda[j], n_batch=n_batch, seq=seq,
                          name=f"lru_{i}")
            x = _res_matmul(y, w_out_i, x, tm=512, tn=1024, name=f"mix_out_{i}")
        else:
            z = _norm_matmul(x, g_mix[i][None], w_in_i, tm=512, tn=1024, gelu_cols=2 * width, out_dtype=BF16,
                             name=f"mix_in_{i}")
            x = _gmlp_out(z, x, sg_norm[j][None], sg_w[j].astype(BF16), sg_b[j].T, w_out_i, tm=256,
                          name=f"gmlp_{i}")

        kv = _norm_matmul(mem, g_mem[i][None], xa_wkv[i].astype(BF16), tm=n_mem, tn=1024, gelu_cols=0,
                          out_dtype=BF16, name=f"mem_kv_{i}")
        x = _xattn(x, g_xattn[i][None], xa_wq[i].astype(BF16), kv, xa_wo[i].astype(BF16), seq=seq, n_mem=n_mem,
                   tm=512, name=f"xattn_{i}")
        x = _moe(x, g_moe[i][None], router_wg[i], router_bg[i], router_we[i], router_be[i],
                 moe_w_up, moe_w_down, layer=i)

    y_prompt = _final_norm(x, g_final[None], row0=0, n_rows=bp * seq, tm=512, name="final_prompt")
    y_sample = _final_norm(x, g_final[None], row0=bp * seq, n_rows=bs * seq, tm=512, name="final_sample")
    return (y_prompt.reshape(bp, seq, d), y_sample.reshape(bs, seq, d))
```

```python
import functools

import jax
import jax.numpy as jnp
from jax import lax
from jax.experimental import pallas as pl
from jax.experimental.pallas import tpu as pltpu

F32 = jnp.float32
BF16 = jnp.bfloat16
U32 = jnp.uint32
I32 = jnp.int32

EPS = 1e-6
LRU_C = 8.0
LOG2_E = 1.4426950408889634
TINY = 1e-30
HEAD_W = 128
CHUNK = 128
CONV_W = 4
XA_HEADS = 4
XA_HEAD_DIM = 128
N_GROUPS = 8
EXPERTS_PER_GROUP = 8
N_EXPERTS = 64
ROUTE_LANES = 128
EXP_LANE0 = N_GROUPS

VMEM_LIMIT_BYTES = 56 * 1024 * 1024
SUBLANES = 8

MOE_BM = 256
HI_MASK = 0xFFFF0000


def _cparams(semantics):
    return pltpu.CompilerParams(dimension_semantics=semantics, vmem_limit_bytes=VMEM_LIMIT_BYTES)


def _gelu(x):
    return 0.5 * x * (1.0 + jnp.tanh(0.7978845608028654 * (x + 0.044715 * (x * x * x))))


def _sigmoid(x):
    return 0.5 * jnp.tanh(0.5 * x) + 0.5


def _rms(x, g):
    ms = jnp.mean(x * x, axis=-1, keepdims=True)
    return x * lax.rsqrt(ms + EPS) * g


def _pack_bf16_pairs(x):
    n = x.shape[1] // 2
    bits = lax.bitcast_convert_type(x.astype(BF16).astype(F32), U32)
    return (bits[:, :n] >> 16) | (bits[:, n:] & jnp.uint32(HI_MASK))


def _unpack_lo(w):
    return lax.bitcast_convert_type(w << 16, F32)


def _unpack_hi(w):
    return lax.bitcast_convert_type(w & jnp.uint32(HI_MASK), F32)


def _norm_matmul_body(x_ref, g_ref, w_ref, o_ref, *, tn, n_gelu):
    h = _rms(x_ref[...], g_ref[...]).astype(BF16)
    for jc in range(w_ref.shape[1] // tn):
        cs = slice(jc * tn, (jc + 1) * tn)
        z = jnp.dot(h, w_ref[:, cs], preferred_element_type=F32)
        o_ref[:, cs] = (_gelu(z) if jc < n_gelu else z).astype(o_ref.dtype)


def _norm_matmul(x, g, w, *, tm, tn, gelu_cols, out_dtype, name):
    t, d = x.shape
    n = w.shape[1]
    tm = min(tm, t)
    tn = min(tn, n)
    assert t % tm == 0 and n % tn == 0 and gelu_cols % tn == 0
    return pl.pallas_call(
        functools.partial(_norm_matmul_body, tn=tn, n_gelu=gelu_cols // tn),
        grid=(t // tm,),
        in_specs=[
            pl.BlockSpec((tm, d), lambda i: (i, 0)),
            pl.BlockSpec((1, d), lambda i: (0, 0)),
            pl.BlockSpec((d, n), lambda i: (0, 0)),
        ],
        out_specs=pl.BlockSpec((tm, n), lambda i: (i, 0)),
        out_shape=jax.ShapeDtypeStruct((t, n), out_dtype),
        compiler_params=_cparams(("parallel",)),
        name=name,
    )(x, g, w)


def _res_matmul_body(y_ref, w_ref, x_ref, o_ref, *, tn):
    y = y_ref[...]
    for jc in range(w_ref.shape[1] // tn):
        cs = slice(jc * tn, (jc + 1) * tn)
        o_ref[:, cs] = x_ref[:, cs] + jnp.dot(y, w_ref[:, cs], preferred_element_type=F32)


def _res_matmul(y, w, x, *, tm, tn, name):
    t, k = y.shape
    n = w.shape[1]
    tm = min(tm, t)
    assert t % tm == 0 and n % tn == 0
    return pl.pallas_call(
        functools.partial(_res_matmul_body, tn=tn),
        grid=(t // tm,),
        in_specs=[
            pl.BlockSpec((tm, k), lambda i: (i, 0)),
            pl.BlockSpec((k, n), lambda i: (0, 0)),
            pl.BlockSpec((tm, n), lambda i: (i, 0)),
        ],
        out_specs=pl.BlockSpec((tm, n), lambda i: (i, 0)),
        out_shape=jax.ShapeDtypeStruct((t, n), F32),
        compiler_params=_cparams(("parallel",)),
        name=name,
    )(y, w, x)


LRU_CW = 256
LRU_ROWS = 256
LRU_PAD = 8


def _lru_body(zg_ref, zx_ref, cw_ref, cb_ref, w_ref, b_ref, lam_ref, o_ref,
              xf_ref, af_ref, bf_ref, ab_ref, bb_ref, *, seq):
    cw = LRU_CW
    rows = LRU_ROWS
    zero_pad = jnp.zeros((LRU_PAD, cw), F32)
    xf_ref[0:LRU_PAD, :] = zero_pad
    xf_ref[LRU_PAD + seq:LRU_PAD + seq + LRU_PAD, :] = zero_pad
    xf_ref[LRU_PAD:LRU_PAD + seq, :] = zx_ref[...].astype(F32)

    nl = -lam_ref[...]
    softplus = jnp.maximum(nl, 0.0) + jnp.log1p(jnp.exp(-jnp.abs(nl)))
    decay = (-LRU_C * LOG2_E) * softplus
    cwv = cw_ref[...]
    cbv = cb_ref[...]
    dirs = ((af_ref, bf_ref), (ab_ref, bb_ref))

    def gate_chunk(c, carry):
        t0 = pl.multiple_of(c * rows, rows)
        xw = xf_ref[pl.ds(t0, rows + 2 * LRU_PAD), :]
        xc = cbv
        for k in range(CONV_W):
            off = LRU_PAD + k - CONV_W // 2
            xc = xc + xw[off:off + rows, :] * cwv[k:k + 1, :]
        for hh in range(cw // HEAD_W):
            sl = slice(hh * HEAD_W, (hh + 1) * HEAD_W)
            xch = xc[:, sl]
            gates = jnp.dot(xch.astype(BF16), w_ref[hh], preferred_element_type=F32) + b_ref[hh]
            for d, (a_ref, b_ref_d) in enumerate(dirs):
                r = _sigmoid(gates[:, (2 * d) * HEAD_W:(2 * d + 1) * HEAD_W])
                i = _sigmoid(gates[:, (2 * d + 1) * HEAD_W:(2 * d + 2) * HEAD_W])
                a = jnp.exp2(r * decay[d:d + 1, sl])
                s = 1.0 - a * a
                a_ref[pl.ds(t0, rows), sl] = a
                b_ref_d[pl.ds(t0, rows), sl] = (s * lax.rsqrt(jnp.maximum(s, TINY))) * (i * xch)
        return carry

    lax.fori_loop(0, seq // rows, gate_chunk, 0)

    row = lax.broadcasted_iota(I32, (SUBLANES, cw), 0)
    n_tiles = seq // SUBLANES

    def scan_tile(k, carry):
        cf, cb = carry
        tf = pl.multiple_of(k * SUBLANES, SUBLANES)
        a = af_ref[pl.ds(tf, SUBLANES), :]
        b = bf_ref[pl.ds(tf, SUBLANES), :]
        for s in (1, 2, 4):
            m = row >= s
            b = jnp.where(m, a * pltpu.roll(b, s, 0) + b, b)
            a = jnp.where(m, a * pltpu.roll(a, s, 0), a)
        h = a * cf + b
        bf_ref[pl.ds(tf, SUBLANES), :] = h
        cf = jnp.broadcast_to(h[SUBLANES - 1:SUBLANES, :], (SUBLANES, cw))
        tb = pl.multiple_of((n_tiles - 1 - k) * SUBLANES, SUBLANES)
        a = ab_ref[pl.ds(tb, SUBLANES), :]
        b = bb_ref[pl.ds(tb, SUBLANES), :]
        for s in (1, 2, 4):
            m = row < SUBLANES - s
            b = jnp.where(m, a * pltpu.roll(b, SUBLANES - s, 0) + b, b)
            a = jnp.where(m, a * pltpu.roll(a, SUBLANES - s, 0), a)
        h = a * cb + b
        bb_ref[pl.ds(tb, SUBLANES), :] = h
        cb = jnp.broadcast_to(h[0:1, :], (SUBLANES, cw))
        return cf, cb

    zero_tile = jnp.zeros((SUBLANES, cw), F32)
    lax.fori_loop(0, n_tiles, scan_tile, (zero_tile, zero_tile), unroll=4)

    def out_chunk(c, carry):
        t0 = pl.multiple_of(c * rows, rows)
        y = zg_ref[pl.ds(t0, rows), :].astype(F32) * (bf_ref[pl.ds(t0, rows), :] + bb_ref[pl.ds(t0, rows), :])
        o_ref[pl.ds(t0, rows), :] = y.astype(BF16)
        return carry

    lax.fori_loop(0, seq // rows, out_chunk, 0)


def _lru_core(z, conv_w, conv_b, wcat, bcat, lam, *, n_batch, seq, name):
    t, two_w = z.shape
    width = two_w // 2
    cw = LRU_CW
    n_ct = width // cw
    hpt = cw // HEAD_W
    assert seq % LRU_ROWS == 0 and width % cw == 0
    return pl.pallas_call(
        functools.partial(_lru_body, seq=seq),
        grid=(n_batch, n_ct),
        in_specs=[
            pl.BlockSpec((seq, cw), lambda b, c: (b, c)),
            pl.BlockSpec((seq, cw), lambda b, c: (b, n_ct + c)),
            pl.BlockSpec((CONV_W, cw), lambda b, c: (0, c)),
            pl.BlockSpec((1, cw), lambda b, c: (0, c)),
            pl.BlockSpec((hpt, HEAD_W, 4 * HEAD_W), lambda b, c: (c, 0, 0)),
            pl.BlockSpec((hpt, 1, 4 * HEAD_W), lambda b, c: (c, 0, 0)),
            pl.BlockSpec((2, cw), lambda b, c: (0, c)),
        ],
        out_specs=pl.BlockSpec((seq, cw), lambda b, c: (b, c)),
        out_shape=jax.ShapeDtypeStruct((t, width), BF16),
        scratch_shapes=[
            pltpu.VMEM((seq + 2 * LRU_PAD, cw), F32),
            pltpu.VMEM((seq, cw), F32),
            pltpu.VMEM((seq, cw), F32),
            pltpu.VMEM((seq, cw), F32),
            pltpu.VMEM((seq, cw), F32),
        ],
        compiler_params=_cparams(("parallel", "parallel")),
        name=name,
    )(z, z, conv_w, conv_b, wcat, bcat, lam)


def _gmlp_body(z_ref, x_ref, sgn_ref, sgw_ref, sgbt_ref, w_ref, o_ref, vn_ref, y_ref, *, tm, width):
    v = z_ref[:, width:].astype(F32)
    vn_ref[...] = _rms(v, sgn_ref[...]).astype(BF16)
    for n in range(tm // CHUNK):
        rs = slice(n * CHUNK, (n + 1) * CHUNK)
        for h in range(width // HEAD_W):
            cs = slice(h * HEAD_W, (h + 1) * HEAD_W)
            mixed = jnp.dot(sgw_ref[h], vn_ref[rs, cs], preferred_element_type=F32) + sgbt_ref[:, h:h + 1]
            y_ref[rs, cs] = (z_ref[rs, cs].astype(F32) * mixed).astype(BF16)
    o_ref[...] = x_ref[...] + jnp.dot(y_ref[...], w_ref[...], preferred_element_type=F32)


def _gmlp_out(z, x, sg_norm, sg_w, sg_bt, w_out, *, tm, name):
    t, two_w = z.shape
    width = two_w // 2
    d = w_out.shape[1]
    n_heads = width // HEAD_W
    tm = min(tm, t)
    assert t % tm == 0 and tm % CHUNK == 0
    return pl.pallas_call(
        functools.partial(_gmlp_body, tm=tm, width=width),
        grid=(t // tm,),
        in_specs=[
            pl.BlockSpec((tm, two_w), lambda i: (i, 0)),
            pl.BlockSpec((tm, d), lambda i: (i, 0)),
            pl.BlockSpec((1, width), lambda i: (0, 0)),
            pl.BlockSpec((n_heads, CHUNK, CHUNK), lambda i: (0, 0, 0)),
            pl.BlockSpec((CHUNK, n_heads), lambda i: (0, 0)),
            pl.BlockSpec((width, d), lambda i: (0, 0)),
        ],
        out_specs=pl.BlockSpec((tm, d), lambda i: (i, 0)),
        out_shape=jax.ShapeDtypeStruct((t, d), F32),
        scratch_shapes=[pltpu.VMEM((tm, width), BF16), pltpu.VMEM((tm, width), BF16)],
        compiler_params=_cparams(("parallel",)),
        name=name,
    )(z, x, sg_norm, sg_w, sg_bt, w_out)


def _xattn_body(x_ref, g_ref, wq_ref, kv_ref, wo_ref, o_ref):
    x = x_ref[...]
    h = _rms(x, g_ref[...]).astype(BF16)
    q = jnp.dot(h, wq_ref[...], preferred_element_type=F32) * (XA_HEAD_DIM ** -0.5)
    qb = q.astype(BF16)
    xa_w = XA_HEADS * XA_HEAD_DIM
    outs = []
    for hd in range(XA_HEADS):
        cs = slice(hd * XA_HEAD_DIM, (hd + 1) * XA_HEAD_DIM)
        vs = slice(xa_w + hd * XA_HEAD_DIM, xa_w + (hd + 1) * XA_HEAD_DIM)
        s = lax.dot_general(qb[:, cs], kv_ref[:, cs], (((1,), (1,)), ((), ())), preferred_element_type=F32)
        p = jnp.exp(s - jnp.max(s, axis=-1, keepdims=True))
        l = jnp.sum(p, axis=-1, keepdims=True)
        o = jnp.dot(p.astype(BF16), kv_ref[:, vs], preferred_element_type=F32) / l
        outs.append(o.astype(BF16))
    o = jnp.concatenate(outs, axis=1)
    o_ref[...] = x + jnp.dot(o, wo_ref[...], preferred_element_type=F32)


def _xattn(x, g, wq, kv, wo, *, seq, n_mem, tm, name):
    t, d = x.shape
    xa_w = wq.shape[1]
    tm = min(tm, seq)
    assert seq % tm == 0
    tiles_per_seq = seq // tm
    return pl.pallas_call(
        _xattn_body,
        grid=(t // tm,),
        in_specs=[
            pl.BlockSpec((tm, d), lambda i: (i, 0)),
            pl.BlockSpec((1, d), lambda i: (0, 0)),
            pl.BlockSpec((d, xa_w), lambda i: (0, 0)),
            pl.BlockSpec((n_mem, 2 * xa_w), lambda i: (i // tiles_per_seq, 0)),
            pl.BlockSpec((xa_w, d), lambda i: (0, 0)),
        ],
        out_specs=pl.BlockSpec((tm, d), lambda i: (i, 0)),
        out_shape=jax.ShapeDtypeStruct((t, d), F32),
        compiler_params=_cparams(("parallel",)),
        name=name,
    )(x, g, wq, kv, wo)


def _router_body(x_ref, g_ref, wcat_ref, whi_ref, b_ref, tri_ref, hp_ref, route_ref, cnt_ref, base_ref):
    @pl.when(pl.program_id(0) == 0)
    def _():
        base_ref[...] = jnp.zeros_like(base_ref)

    h32 = _rms(x_ref[...], g_ref[...])
    hhi = h32.astype(BF16)
    hlo = (h32 - hhi.astype(F32)).astype(BF16)
    hp_ref[...] = _pack_bf16_pairs(h32)

    p = jnp.dot(hhi, wcat_ref[...], preferred_element_type=F32)
    lg = p[:, :ROUTE_LANES] + p[:, ROUTE_LANES:] + jnp.dot(hlo, whi_ref[...], preferred_element_type=F32)
    lg = lg + b_ref[...]

    tm = lg.shape[0]
    lane = lax.broadcasted_iota(I32, (tm, ROUTE_LANES), 1)
    lane_f = lane.astype(F32)
    neg = jnp.float32(-1e30)
    big = jnp.float32(ROUTE_LANES)

    def first_max(vals):
        m = jnp.max(vals, axis=-1, keepdims=True)
        idx = jnp.min(jnp.where(vals == m, lane_f, big), axis=-1, keepdims=True)
        return m, idx

    gmask = lane < N_GROUPS
    mg, g_sel = first_max(jnp.where(gmask, lg, neg))
    gate_g = 1.0 / jnp.sum(jnp.where(gmask, jnp.exp(lg - mg), 0.0), axis=-1, keepdims=True)

    grp_of_lane = ((lane - EXP_LANE0) >> 3).astype(F32)
    emask = (lane >= EXP_LANE0) & (lane < EXP_LANE0 + N_EXPERTS) & (grp_of_lane == g_sel)
    le = jnp.where(emask, lg, neg)
    m1, i1 = first_max(le)
    m2, i2 = first_max(jnp.where(lane_f == i1, neg, le))
    e21 = jnp.exp(m2 - m1)
    w1 = 1.0 / (1.0 + e21)
    w2 = e21 * w1
    g1 = gate_g * w1
    g2 = gate_g * w2

    is1 = lane_f == i1
    is2 = lane_f == i2
    onehot = (is1 | is2).astype(F32)
    before = jnp.dot(tri_ref[...], onehot.astype(BF16), preferred_element_type=F32) + base_ref[0:1, :]
    r1 = jnp.sum(jnp.where(is1, before, 0.0), axis=-1, keepdims=True)
    r2 = jnp.sum(jnp.where(is2, before, 0.0), axis=-1, keepdims=True)
    new_base = base_ref[...] + jnp.sum(onehot, axis=0, keepdims=True)
    base_ref[...] = new_base
    cnt_ref[...] = new_base

    e1 = i1 - float(EXP_LANE0)
    e2 = i2 - float(EXP_LANE0)
    slab = jnp.where(lane == 0, e1, jnp.where(lane == 1, e2, jnp.where(lane == 2, g1, jnp.where(
        lane == 3, g2, jnp.where(lane == 4, r1, jnp.where(lane == 5, r2, 0.0))))))
    route_ref[...] = slab


def _router(x, g, wcat, whi, bias, tri, *, tm, name):
    t, d = x.shape
    assert t % tm == 0
    return pl.pallas_call(
        _router_body,
        grid=(t // tm,),
        in_specs=[
            pl.BlockSpec((tm, d), lambda i: (i, 0)),
            pl.BlockSpec((1, d), lambda i: (0, 0)),
            pl.BlockSpec((d, 2 * ROUTE_LANES), lambda i: (0, 0)),
            pl.BlockSpec((d, ROUTE_LANES), lambda i: (0, 0)),
            pl.BlockSpec((1, ROUTE_LANES), lambda i: (0, 0)),
            pl.BlockSpec((tm, tm), lambda i: (0, 0)),
        ],
        out_specs=[
            pl.BlockSpec((tm, d // 2), lambda i: (i, 0)),
            pl.BlockSpec((tm, ROUTE_LANES), lambda i: (i, 0)),
            pl.BlockSpec((SUBLANES, ROUTE_LANES), lambda i: (0, 0)),
        ],
        out_shape=[
            jax.ShapeDtypeStruct((t, d // 2), U32),
            jax.ShapeDtypeStruct((t, ROUTE_LANES), F32),
            jax.ShapeDtypeStruct((SUBLANES, ROUTE_LANES), F32),
        ],
        scratch_shapes=[pltpu.VMEM((SUBLANES, ROUTE_LANES), F32)],
        compiler_params=_cparams(("arbitrary",)),
        name=name,
    )(x, g, wcat, whi, bias, tri)


def _row_copies_wait(src_ref, dst_ref, sem):
    pltpu.make_async_copy(src_ref, dst_ref, sem).wait()


CAST_ROWS = 256


def _largest_divisor(n, candidates=(8, 4, 2, 1)):
    return next(c for c in candidates if n % c == 0)


def _stage_expert_weight(plan_ref, nb_ref, w_hbm, stage_ref, wb_ref, wsem, *, layer):
    i = pl.program_id(0)

    def w_copy(e, s):
        return pltpu.make_async_copy(w_hbm.at[layer, e], stage_ref.at[s], wsem.at[s])

    @pl.when(i == 0)
    def _():
        w_copy(plan_ref[0, 0], 0).start()

    @pl.when(jnp.logical_and(i < nb_ref[0], plan_ref[1, i] == 1))
    def _():
        s = plan_ref[2, i]
        w_copy(plan_ref[0, i], s).wait()

        @pl.when(plan_ref[4, i] == 1)
        def _():
            w_copy(plan_ref[3, i], 1 - s).start()

        def cast_rows(c, carry):
            r0 = pl.multiple_of(c * CAST_ROWS, CAST_ROWS)
            wb_ref[pl.ds(r0, CAST_ROWS), :] = stage_ref[s, pl.ds(r0, CAST_ROWS), :].astype(BF16)
            return carry

        lax.fori_loop(0, wb_ref.shape[0] // CAST_ROWS, cast_rows, 0)


def _expert_up_body(plan_ref, nb_ref, hp_hbm, rowtok_hbm, wu_hbm, hh_ref,
                    stage_ref, wub_ref, rows_ref, idx_smem, wsem, rsem, isem, *, layer, idx_blocks):
    bm = MOE_BM
    i = pl.program_id(0)
    n_used = nb_ref[0]
    used = i < n_used
    _stage_expert_weight(plan_ref, nb_ref, wu_hbm, stage_ref, wub_ref, wsem, layer=layer)

    def load_idx(group):
        cp = pltpu.make_async_copy(rowtok_hbm.at[pl.ds(group * (idx_blocks * bm), idx_blocks * bm)], idx_smem, isem)
        cp.start()
        cp.wait()

    def issue_rows(blk, s):
        base = (blk % idx_blocks) * bm
        for r in range(bm):
            tok = idx_smem[base + r]
            pltpu.make_async_copy(hp_hbm.at[pl.ds(tok, 1)], rows_ref.at[s, pl.ds(r, 1)], rsem.at[s]).start()

    @pl.when(i == 0)
    def _():
        load_idx(0)
        issue_rows(0, 0)

    nxt = i + 1

    @pl.when(nxt < n_used)
    def _():
        @pl.when(nxt % idx_blocks == 0)
        def _():
            load_idx(nxt // idx_blocks)

        for s in range(2):
            @pl.when(nxt % 2 == s)
            def _():
                issue_rows(nxt, s)

    @pl.when(jnp.logical_not(used))
    def _():
        hh_ref[...] = jnp.zeros_like(hh_ref)

    @pl.when(used)
    def _():
        s = i % 2
        _row_copies_wait(hp_hbm.at[pl.ds(0, bm)], rows_ref.at[s], rsem.at[s])
        w = rows_ref[s]
        half = w.shape[1]
        hidden = hh_ref.shape[1]
        hu = jnp.dot(_unpack_lo(w).astype(BF16), wub_ref[:half, :], preferred_element_type=F32)
        hu = hu + jnp.dot(_unpack_hi(w).astype(BF16), wub_ref[half:, :], preferred_element_type=F32)
        gate = hu[:, :hidden]
        hh_ref[...] = (gate * jax.nn.sigmoid(gate) * hu[:, hidden:]).astype(BF16)


def _expert_down_body(plan_ref, nb_ref, hh_ref, wd_hbm, ys_ref, stage_ref, wdb_ref, wsem, *, layer):
    _stage_expert_weight(plan_ref, nb_ref, wd_hbm, stage_ref, wdb_ref, wsem, layer=layer)
    used = pl.program_id(0) < nb_ref[0]

    @pl.when(jnp.logical_not(used))
    def _():
        ys_ref[...] = jnp.zeros_like(ys_ref)

    @pl.when(used)
    def _():
        ys_ref[...] = _pack_bf16_pairs(jnp.dot(hh_ref[...], wdb_ref[...], preferred_element_type=F32))


def _experts(plan, n_used, hp, row_tok, w_up, w_down, *, layer, name):
    n_rows = row_tok.shape[0]
    dw = hp.shape[1]
    n_blocks = n_rows // MOE_BM
    idx_blocks = _largest_divisor(n_blocks)
    _, _, d, two_f = w_up.shape
    hidden = w_down.shape[2]
    any_spec = pl.BlockSpec(memory_space=pl.ANY)

    def row_map(i, plan_, nb):
        return (jnp.minimum(i, nb[0] - 1), 0)

    def out_map(i, plan_, nb):
        return (i, 0)

    hh = pl.pallas_call(
        functools.partial(_expert_up_body, layer=layer, idx_blocks=idx_blocks),
        grid_spec=pltpu.PrefetchScalarGridSpec(
            num_scalar_prefetch=2,
            grid=(n_blocks,),
            in_specs=[any_spec, any_spec, any_spec],
            out_specs=pl.BlockSpec((MOE_BM, hidden), out_map),
            scratch_shapes=[
                pltpu.VMEM((2, d, two_f), F32),
                pltpu.VMEM((d, two_f), BF16),
                pltpu.VMEM((2, MOE_BM, dw), U32),
                pltpu.SMEM((idx_blocks * MOE_BM,), I32),
                pltpu.SemaphoreType.DMA((2,)),
                pltpu.SemaphoreType.DMA((2,)),
                pltpu.SemaphoreType.DMA,
            ],
        ),
        out_shape=jax.ShapeDtypeStruct((n_rows, hidden), BF16),
        compiler_params=_cparams(("arbitrary",)),
        name=f"{name}_up",
    )(plan, n_used, hp, row_tok, w_up)
    return pl.pallas_call(
        functools.partial(_expert_down_body, layer=layer),
        grid_spec=pltpu.PrefetchScalarGridSpec(
            num_scalar_prefetch=2,
            grid=(n_blocks,),
            in_specs=[pl.BlockSpec((MOE_BM, hidden), row_map), any_spec],
            out_specs=pl.BlockSpec((MOE_BM, dw), out_map),
            scratch_shapes=[
                pltpu.VMEM((2, hidden, d), F32),
                pltpu.VMEM((hidden, d), BF16),
                pltpu.SemaphoreType.DMA((2,)),
            ],
        ),
        out_shape=jax.ShapeDtypeStruct((n_rows, dw), U32),
        compiler_params=_cparams(("arbitrary",)),
        name=f"{name}_down",
    )(plan, n_used, hh, w_down)


def _combine_body(x_ref, route_ref, dest_hbm, ys_hbm, o_ref, idx_smem, rows_ref, isem, rsem, *, tm, idx_tiles):
    i = pl.program_id(0)
    per_tile = 2 * tm

    def load_idx(group):
        cp = pltpu.make_async_copy(dest_hbm.at[pl.ds(group * (idx_tiles * per_tile), idx_tiles * per_tile)],
                                   idx_smem, isem)
        cp.start()
        cp.wait()

    def issue_rows(tile, s):
        base = (tile % idx_tiles) * per_tile
        for t in range(tm):
            for k in range(2):
                d = idx_smem[base + 2 * t + k]
                pltpu.make_async_copy(ys_hbm.at[pl.ds(d, 1)], rows_ref.at[s, k, pl.ds(t, 1)],
                                      rsem.at[s]).start(priority=k)

    @pl.when(i == 0)
    def _():
        load_idx(0)
        issue_rows(0, 0)

    nxt = i + 1

    @pl.when(nxt < pl.num_programs(0))
    def _():
        @pl.when(nxt % idx_tiles == 0)
        def _():
            load_idx(nxt // idx_tiles)

        for s in range(2):
            @pl.when(nxt % 2 == s)
            def _():
                issue_rows(nxt, s)

    s = i % 2
    for k in range(2):
        _row_copies_wait(ys_hbm.at[pl.ds(0, tm)], rows_ref.at[s, k], rsem.at[s])

    half = rows_ref.shape[3]
    g0 = route_ref[:, 2:3]
    g1 = route_ref[:, 3:4]
    w0 = rows_ref[s, 0]
    w1 = rows_ref[s, 1]
    o_ref[:, :half] = x_ref[:, :half] + (g0 * _unpack_lo(w0) + g1 * _unpack_lo(w1))
    o_ref[:, half:] = x_ref[:, half:] + (g0 * _unpack_hi(w0) + g1 * _unpack_hi(w1))


def _combine(x, route, dest, ys, *, tm, name):
    t, d = x.shape
    dw = ys.shape[1]
    assert t % tm == 0
    idx_tiles = _largest_divisor(t // tm)
    return pl.pallas_call(
        functools.partial(_combine_body, tm=tm, idx_tiles=idx_tiles),
        grid=(t // tm,),
        in_specs=[
            pl.BlockSpec((tm, d), lambda i: (i, 0)),
            pl.BlockSpec((tm, ROUTE_LANES), lambda i: (i, 0)),
            pl.BlockSpec(memory_space=pl.ANY),
            pl.BlockSpec(memory_space=pl.ANY),
        ],
        out_specs=pl.BlockSpec((tm, d), lambda i: (i, 0)),
        out_shape=jax.ShapeDtypeStruct((t, d), F32),
        scratch_shapes=[
            pltpu.SMEM((idx_tiles * 2 * tm,), I32),
            pltpu.VMEM((2, 2, tm, dw), U32),
            pltpu.SemaphoreType.DMA,
            pltpu.SemaphoreType.DMA((2,)),
        ],
        compiler_params=_cparams(("arbitrary",)),
        name=name,
    )(x, route, dest, ys)


def _final_norm_body(x_ref, g_ref, o_ref):
    o_ref[...] = _rms(x_ref[...], g_ref[...])


def _final_norm(x, g, *, row0, n_rows, tm, name):
    d = x.shape[1]
    tm = min(tm, n_rows)
    assert n_rows % tm == 0 and row0 % tm == 0
    blk0 = row0 // tm
    return pl.pallas_call(
        _final_norm_body,
        grid=(n_rows // tm,),
        in_specs=[pl.BlockSpec((tm, d), lambda i: (blk0 + i, 0)), pl.BlockSpec((1, d), lambda i: (0, 0))],
        out_specs=pl.BlockSpec((tm, d), lambda i: (i, 0)),
        out_shape=jax.ShapeDtypeStruct((n_rows, d), F32),
        compiler_params=_cparams(("parallel",)),
        name=name,
    )(x, g)


def _moe(x, g, wg, bg, we, be, w_up, w_down, *, layer):
    tag = str(layer)
    t, d = x.shape
    tm_route = min(512, t)
    tm_rows = min(256, t)

    wr = jnp.zeros((d, ROUTE_LANES), F32).at[:, :N_GROUPS].set(wg).at[:, EXP_LANE0:EXP_LANE0 + N_EXPERTS].set(we)
    wr_hi = wr.astype(BF16)
    wr_lo = (wr - wr_hi.astype(F32)).astype(BF16)
    bias = jnp.zeros((1, ROUTE_LANES), F32).at[0, :N_GROUPS].set(bg).at[0, EXP_LANE0:EXP_LANE0 + N_EXPERTS].set(be)
    ii = lax.broadcasted_iota(I32, (tm_route, tm_route), 0)
    jj = lax.broadcasted_iota(I32, (tm_route, tm_route), 1)
    tri = (jj < ii).astype(BF16)

    hp, route, cnt = _router(x, g, jnp.concatenate([wr_hi, wr_lo], axis=1), wr_hi, bias, tri,
                             tm=tm_route, name=f"router_{tag}")

    counts = cnt[0, EXP_LANE0:EXP_LANE0 + N_EXPERTS].astype(I32)
    padded = ((counts + MOE_BM - 1) // MOE_BM) * MOE_BM
    pad_end = jnp.cumsum(padded)
    pad_start = pad_end - padded
    e_ids = route[:, 0:2].astype(I32)
    rank = route[:, 4:6].astype(I32)
    dest = (pad_start[e_ids] + rank).reshape(-1)
    n_blocks = (2 * t) // MOE_BM + N_EXPERTS
    n_used = (pad_end[-1] // MOE_BM).astype(I32)
    blk = jnp.arange(n_blocks, dtype=I32)
    block_e = jnp.sum((pad_end[None, :] <= (jnp.minimum(blk, n_used - 1) * MOE_BM)[:, None]).astype(I32), axis=1)
    fresh = jnp.concatenate([jnp.ones((1,), I32), (block_e[1:] != block_e[:-1]).astype(I32)])
    slot = (jnp.cumsum(fresh) - 1) % 2
    next_blk = pad_end[block_e] // MOE_BM
    has_next = (next_blk < n_used).astype(I32)
    next_e = block_e[jnp.minimum(next_blk, n_used - 1)]
    plan = jnp.stack([block_e, fresh, slot, next_e, has_next]).astype(I32)
    row_tok = jnp.zeros((n_blocks * MOE_BM,), I32).at[dest].set(jnp.repeat(jnp.arange(t, dtype=I32), 2))

    ys = _experts(plan, n_used.reshape(1), hp, row_tok, w_up, w_down, layer=layer, name=f"experts_{tag}")
    return _combine(x, route, dest, ys, tm=tm_rows, name=f"combine_{tag}")


def kernel(x_prompt, x_sample, mem_prompt, mem_sample, g_mix, w_in, w_out, conv_w, conv_b, lru_wa, lru_ba, lru_wx, lru_bx, lru_lambda, sg_norm, sg_w, sg_b, g_xattn, g_mem, xa_wq, xa_wkv, xa_wo, g_moe, router_wg, router_bg, router_we, router_be, moe_w_up, moe_w_down, g_final):
    bp, seq, d = x_prompt.shape
    bs = x_sample.shape[0]
    assert x_sample.shape[1] == seq
    n_batch = bp + bs
    t = n_batch * seq
    n_mem = mem_prompt.shape[1]
    depth = w_in.shape[0]
    width = w_out.shape[1]
    n_heads = width // HEAD_W

    x = jnp.concatenate([x_prompt, x_sample], axis=0).reshape(t, d)
    mem = jnp.concatenate([mem_prompt, mem_sample], axis=0).reshape(n_batch * n_mem, d)

    for i in range(depth):
        j = i // 2
        w_in_i = w_in[i].astype(BF16)
        w_out_i = w_out[i].astype(BF16)
        if i % 2 == 0:
            z = _norm_matmul(x, g_mix[i][None], w_in_i, tm=512, tn=1024, gelu_cols=width, out_dtype=BF16,
                             name=f"mix_in_{i}")
            wcat = jnp.concatenate([lru_wa[j, 0], lru_wx[j, 0], lru_wa[j, 1], lru_wx[j, 1]], axis=-1).astype(BF16)
            bcat = jnp.concatenate([lru_ba[j, 0].reshape(n_heads, 1, HEAD_W), lru_bx[j, 0].reshape(n_heads, 1, HEAD_W),
                                    lru_ba[j, 1].reshape(n_heads, 1, HEAD_W), lru_bx[j, 1].reshape(n_heads, 1, HEAD_W)],
                                   axis=-1)
            y = _lru_core(z, conv_w[j], conv_b[j][None], wcat, bcat, lru_lambda[j], n_batch=n_batch, seq=seq,
                          name=f"lru_{i}")
            x = _res_matmul(y, w_out_i, x, tm=512, tn=1024, name=f"mix_out_{i}")
        else:
            z = _norm_matmul(x, g_mix[i][None], w_in_i, tm=512, tn=1024, gelu_cols=2 * width, out_dtype=BF16,
                             name=f"mix_in_{i}")
            x = _gmlp_out(z, x, sg_norm[j][None], sg_w[j].astype(BF16), sg_b[j].T, w_out_i, tm=256,
                          name=f"gmlp_{i}")

        kv = _norm_matmul(mem, g_mem[i][None], xa_wkv[i].astype(BF16), tm=n_mem, tn=1024, gelu_cols=0,
                          out_dtype=BF16, name=f"mem_kv_{i}")
        x = _xattn(x, g_xattn[i][None], xa_wq[i].astype(BF16), kv, xa_wo[i].astype(BF16), seq=seq, n_mem=n_mem,
                   tm=512, name=f"xattn_{i}")
        x = _moe(x, g_moe[i][None], router_wg[i], router_bg[i], router_we[i], router_be[i],
                 moe_w_up, moe_w_down, layer=i)

    y_prompt = _final_norm(x, g_final[None], row0=0, n_rows=bp * seq, tm=512, name="final_prompt")
    y_sample = _final_norm(x, g_final[None], row0=bp * seq, n_rows=bs * seq, tm=512, name="final_sample")
    return (y_prompt.reshape(bp, seq, d), y_sample.reshape(bs, seq, d))
```

```python
import functools

import jax
import jax.numpy as jnp
from jax import lax
from jax.experimental import pallas as pl
from jax.experimental.pallas import tpu as pltpu

F32 = jnp.float32
BF16 = jnp.bfloat16
U32 = jnp.uint32
I32 = jnp.int32

EPS = 1e-6
LRU_C = 8.0
LOG2_E = 1.4426950408889634
TINY = 1e-30
HEAD_W = 128
CHUNK = 128
CONV_W = 4
XA_HEADS = 4
XA_HEAD_DIM = 128
N_GROUPS = 8
EXPERTS_PER_GROUP = 8
N_EXPERTS = 64
ROUTE_LANES = 128
EXP_LANE0 = N_GROUPS

VMEM_LIMIT_BYTES = 56 * 1024 * 1024
SUBLANES = 8

MOE_BM = 256
HI_MASK = 0xFFFF0000


def _cparams(semantics):
    return pltpu.CompilerParams(dimension_semantics=semantics, vmem_limit_bytes=VMEM_LIMIT_BYTES)


def _gelu(x):
    return 0.5 * x * (1.0 + jnp.tanh(0.7978845608028654 * (x + 0.044715 * (x * x * x))))


def _rms(x, g):
    ms = jnp.mean(x * x, axis=-1, keepdims=True)
    return x * lax.rsqrt(ms + EPS) * g


def _pack_bf16_pairs(x):
    n = x.shape[1] // 2
    bits = lax.bitcast_convert_type(x.astype(BF16).astype(F32), U32)
    return (bits[:, :n] >> 16) | (bits[:, n:] & jnp.uint32(HI_MASK))


def _unpack_lo(w):
    return lax.bitcast_convert_type(w << 16, F32)


def _unpack_hi(w):
    return lax.bitcast_convert_type(w & jnp.uint32(HI_MASK), F32)


def _norm_matmul_body(x_ref, g_ref, w_ref, o_ref, *, tn, n_gelu):
    h = _rms(x_ref[...], g_ref[...]).astype(BF16)
    for jc in range(w_ref.shape[1] // tn):
        cs = slice(jc * tn, (jc + 1) * tn)
        z = jnp.dot(h, w_ref[:, cs], preferred_element_type=F32)
        o_ref[:, cs] = (_gelu(z) if jc < n_gelu else z).astype(o_ref.dtype)


def _norm_matmul(x, g, w, *, tm, tn, gelu_cols, out_dtype, name):
    t, d = x.shape
    n = w.shape[1]
    tm = min(tm, t)
    tn = min(tn, n)
    assert t % tm == 0 and n % tn == 0 and gelu_cols % tn == 0
    return pl.pallas_call(
        functools.partial(_norm_matmul_body, tn=tn, n_gelu=gelu_cols // tn),
        grid=(t // tm,),
        in_specs=[
            pl.BlockSpec((tm, d), lambda i: (i, 0)),
            pl.BlockSpec((1, d), lambda i: (0, 0)),
            pl.BlockSpec((d, n), lambda i: (0, 0)),
        ],
        out_specs=pl.BlockSpec((tm, n), lambda i: (i, 0)),
        out_shape=jax.ShapeDtypeStruct((t, n), out_dtype),
        compiler_params=_cparams(("parallel",)),
        name=name,
    )(x, g, w)


def _res_matmul_body(y_ref, w_ref, x_ref, o_ref, *, tn):
    y = y_ref[...]
    for jc in range(w_ref.shape[1] // tn):
        cs = slice(jc * tn, (jc + 1) * tn)
        o_ref[:, cs] = x_ref[:, cs] + jnp.dot(y, w_ref[:, cs], preferred_element_type=F32)


def _res_matmul(y, w, x, *, tm, tn, name):
    t, k = y.shape
    n = w.shape[1]
    tm = min(tm, t)
    assert t % tm == 0 and n % tn == 0
    return pl.pallas_call(
        functools.partial(_res_matmul_body, tn=tn),
        grid=(t // tm,),
        in_specs=[
            pl.BlockSpec((tm, k), lambda i: (i, 0)),
            pl.BlockSpec((k, n), lambda i: (0, 0)),
            pl.BlockSpec((tm, n), lambda i: (i, 0)),
        ],
        out_specs=pl.BlockSpec((tm, n), lambda i: (i, 0)),
        out_shape=jax.ShapeDtypeStruct((t, n), F32),
        compiler_params=_cparams(("parallel",)),
        name=name,
    )(y, w, x)


LRU_CW = 256
LRU_ROWS = 256
LRU_PAD = 8


def _lru_body(zg_ref, zx_ref, cw_ref, cb_ref, w_ref, b_ref, lam_ref, o_ref,
              xf_ref, af_ref, bf_ref, ab_ref, bb_ref, *, seq):
    cw = LRU_CW
    rows = LRU_ROWS
    zero_pad = jnp.zeros((LRU_PAD, cw), F32)
    xf_ref[0:LRU_PAD, :] = zero_pad
    xf_ref[LRU_PAD + seq:LRU_PAD + seq + LRU_PAD, :] = zero_pad
    xf_ref[LRU_PAD:LRU_PAD + seq, :] = zx_ref[...].astype(F32)

    nl = -lam_ref[...]
    softplus = jnp.maximum(nl, 0.0) + jnp.log1p(jnp.exp(-jnp.abs(nl)))
    half_decay = (-0.5 * LRU_C * LOG2_E) * softplus
    cwv = cw_ref[...]
    cbv = cb_ref[...]
    dirs = ((af_ref, bf_ref), (ab_ref, bb_ref))

    def gate_chunk(c, carry):
        t0 = pl.multiple_of(c * rows, rows)
        xw = xf_ref[pl.ds(t0, rows + 2 * LRU_PAD), :]
        xc = cbv
        for k in range(CONV_W):
            off = LRU_PAD + k - CONV_W // 2
            xc = xc + xw[off:off + rows, :] * cwv[k:k + 1, :]
        for hh in range(cw // HEAD_W):
            sl = slice(hh * HEAD_W, (hh + 1) * HEAD_W)
            xch = xc[:, sl]
            th = jnp.tanh(jnp.dot(xch.astype(BF16), w_ref[hh], preferred_element_type=F32) + b_ref[hh])
            half_x = 0.5 * xch
            for d, (a_ref, b_ref_d) in enumerate(dirs):
                th_r = th[:, (2 * d) * HEAD_W:(2 * d + 1) * HEAD_W]
                th_i = th[:, (2 * d + 1) * HEAD_W:(2 * d + 2) * HEAD_W]
                hd = half_decay[d:d + 1, sl]
                a = jnp.exp2(th_r * hd + hd)
                s = 1.0 - a * a
                a_ref[pl.ds(t0, rows), sl] = a
                b_ref_d[pl.ds(t0, rows), sl] = (s * lax.rsqrt(jnp.maximum(s, TINY))) * ((th_i + 1.0) * half_x)
        return carry

    lax.fori_loop(0, seq // rows, gate_chunk, 0)

    row = lax.broadcasted_iota(I32, (SUBLANES, cw), 0)
    n_tiles = seq // SUBLANES

    def scan_tile(k, carry):
        cf, cb = carry
        tf = pl.multiple_of(k * SUBLANES, SUBLANES)
        a = af_ref[pl.ds(tf, SUBLANES), :]
        b = bf_ref[pl.ds(tf, SUBLANES), :]
        for s in (1, 2, 4):
            m = row >= s
            b = jnp.where(m, a * pltpu.roll(b, s, 0) + b, b)
            a = jnp.where(m, a * pltpu.roll(a, s, 0), a)
        h = a * cf + b
        bf_ref[pl.ds(tf, SUBLANES), :] = h
        cf = jnp.broadcast_to(h[SUBLANES - 1:SUBLANES, :], (SUBLANES, cw))
        tb = pl.multiple_of((n_tiles - 1 - k) * SUBLANES, SUBLANES)
        a = ab_ref[pl.ds(tb, SUBLANES), :]
        b = bb_ref[pl.ds(tb, SUBLANES), :]
        for s in (1, 2, 4):
            m = row < SUBLANES - s
            b = jnp.where(m, a * pltpu.roll(b, SUBLANES - s, 0) + b, b)
            a = jnp.where(m, a * pltpu.roll(a, SUBLANES - s, 0), a)
        h = a * cb + b
        bb_ref[pl.ds(tb, SUBLANES), :] = h
        cb = jnp.broadcast_to(h[0:1, :], (SUBLANES, cw))
        return cf, cb

    zero_tile = jnp.zeros((SUBLANES, cw), F32)
    lax.fori_loop(0, n_tiles, scan_tile, (zero_tile, zero_tile), unroll=8)

    def out_chunk(c, carry):
        t0 = pl.multiple_of(c * rows, rows)
        y = zg_ref[pl.ds(t0, rows), :].astype(F32) * (bf_ref[pl.ds(t0, rows), :] + bb_ref[pl.ds(t0, rows), :])
        o_ref[pl.ds(t0, rows), :] = y.astype(BF16)
        return carry

    lax.fori_loop(0, seq // rows, out_chunk, 0)


def _lru_core(z, conv_w, conv_b, wcat, bcat, lam, *, n_batch, seq, name):
    t, two_w = z.shape
    width = two_w // 2
    cw = LRU_CW
    n_ct = width // cw
    hpt = cw // HEAD_W
    assert seq % LRU_ROWS == 0 and width % cw == 0
    return pl.pallas_call(
        functools.partial(_lru_body, seq=seq),
        grid=(n_batch, n_ct),
        in_specs=[
            pl.BlockSpec((seq, cw), lambda b, c: (b, c)),
            pl.BlockSpec((seq, cw), lambda b, c: (b, n_ct + c)),
            pl.BlockSpec((CONV_W, cw), lambda b, c: (0, c)),
            pl.BlockSpec((1, cw), lambda b, c: (0, c)),
            pl.BlockSpec((hpt, HEAD_W, 4 * HEAD_W), lambda b, c: (c, 0, 0)),
            pl.BlockSpec((hpt, 1, 4 * HEAD_W), lambda b, c: (c, 0, 0)),
            pl.BlockSpec((2, cw), lambda b, c: (0, c)),
        ],
        out_specs=pl.BlockSpec((seq, cw), lambda b, c: (b, c)),
        out_shape=jax.ShapeDtypeStruct((t, width), BF16),
        scratch_shapes=[
            pltpu.VMEM((seq + 2 * LRU_PAD, cw), F32),
            pltpu.VMEM((seq, cw), F32),
            pltpu.VMEM((seq, cw), F32),
            pltpu.VMEM((seq, cw), F32),
            pltpu.VMEM((seq, cw), F32),
        ],
        compiler_params=_cparams(("parallel", "parallel")),
        name=name,
    )(z, z, conv_w, conv_b, wcat, bcat, lam)


def _gmlp_body(z_ref, x_ref, sgn_ref, sgw_ref, sgbt_ref, w_ref, o_ref, vn_ref, y_ref, *, tm, width):
    v = z_ref[:, width:].astype(F32)
    vn_ref[...] = _rms(v, sgn_ref[...]).astype(BF16)
    for n in range(tm // CHUNK):
        rs = slice(n * CHUNK, (n + 1) * CHUNK)
        for h in range(width // HEAD_W):
            cs = slice(h * HEAD_W, (h + 1) * HEAD_W)
            mixed = jnp.dot(sgw_ref[h], vn_ref[rs, cs], preferred_element_type=F32) + sgbt_ref[:, h:h + 1]
            y_ref[rs, cs] = (z_ref[rs, cs].astype(F32) * mixed).astype(BF16)
    o_ref[...] = x_ref[...] + jnp.dot(y_ref[...], w_ref[...], preferred_element_type=F32)


def _gmlp_out(z, x, sg_norm, sg_w, sg_bt, w_out, *, tm, name):
    t, two_w = z.shape
    width = two_w // 2
    d = w_out.shape[1]
    n_heads = width // HEAD_W
    tm = min(tm, t)
    assert t % tm == 0 and tm % CHUNK == 0
    return pl.pallas_call(
        functools.partial(_gmlp_body, tm=tm, width=width),
        grid=(t // tm,),
        in_specs=[
            pl.BlockSpec((tm, two_w), lambda i: (i, 0)),
            pl.BlockSpec((tm, d), lambda i: (i, 0)),
            pl.BlockSpec((1, width), lambda i: (0, 0)),
            pl.BlockSpec((n_heads, CHUNK, CHUNK), lambda i: (0, 0, 0)),
            pl.BlockSpec((CHUNK, n_heads), lambda i: (0, 0)),
            pl.BlockSpec((width, d), lambda i: (0, 0)),
        ],
        out_specs=pl.BlockSpec((tm, d), lambda i: (i, 0)),
        out_shape=jax.ShapeDtypeStruct((t, d), F32),
        scratch_shapes=[pltpu.VMEM((tm, width), BF16), pltpu.VMEM((tm, width), BF16)],
        compiler_params=_cparams(("parallel",)),
        name=name,
    )(z, x, sg_norm, sg_w, sg_bt, w_out)


def _xattn_body(x_ref, g_ref, wq_ref, kv_ref, wo_ref, o_ref):
    x = x_ref[...]
    h = _rms(x, g_ref[...]).astype(BF16)
    q = jnp.dot(h, wq_ref[...], preferred_element_type=F32) * (XA_HEAD_DIM ** -0.5)
    qb = q.astype(BF16)
    xa_w = XA_HEADS * XA_HEAD_DIM
    outs = []
    for hd in range(XA_HEADS):
        cs = slice(hd * XA_HEAD_DIM, (hd + 1) * XA_HEAD_DIM)
        vs = slice(xa_w + hd * XA_HEAD_DIM, xa_w + (hd + 1) * XA_HEAD_DIM)
        s = lax.dot_general(qb[:, cs], kv_ref[:, cs], (((1,), (1,)), ((), ())), preferred_element_type=F32)
        p = jnp.exp(s - jnp.max(s, axis=-1, keepdims=True))
        l = jnp.sum(p, axis=-1, keepdims=True)
        o = jnp.dot(p.astype(BF16), kv_ref[:, vs], preferred_element_type=F32) / l
        outs.append(o.astype(BF16))
    o = jnp.concatenate(outs, axis=1)
    o_ref[...] = x + jnp.dot(o, wo_ref[...], preferred_element_type=F32)


def _xattn(x, g, wq, kv, wo, *, seq, n_mem, tm, name):
    t, d = x.shape
    xa_w = wq.shape[1]
    tm = min(tm, seq)
    assert seq % tm == 0
    tiles_per_seq = seq // tm
    return pl.pallas_call(
        _xattn_body,
        grid=(t // tm,),
        in_specs=[
            pl.BlockSpec((tm, d), lambda i: (i, 0)),
            pl.BlockSpec((1, d), lambda i: (0, 0)),
            pl.BlockSpec((d, xa_w), lambda i: (0, 0)),
            pl.BlockSpec((n_mem, 2 * xa_w), lambda i: (i // tiles_per_seq, 0)),
            pl.BlockSpec((xa_w, d), lambda i: (0, 0)),
        ],
        out_specs=pl.BlockSpec((tm, d), lambda i: (i, 0)),
        out_shape=jax.ShapeDtypeStruct((t, d), F32),
        compiler_params=_cparams(("parallel",)),
        name=name,
    )(x, g, wq, kv, wo)


def _router_body(x_ref, g_ref, wcat_ref, whi_ref, b_ref, tri_ref, hp_ref, route_ref, cnt_ref, base_ref):
    @pl.when(pl.program_id(0) == 0)
    def _():
        base_ref[...] = jnp.zeros_like(base_ref)

    h32 = _rms(x_ref[...], g_ref[...])
    hhi = h32.astype(BF16)
    hlo = (h32 - hhi.astype(F32)).astype(BF16)
    hp_ref[...] = _pack_bf16_pairs(h32)

    p = jnp.dot(hhi, wcat_ref[...], preferred_element_type=F32)
    lg = p[:, :ROUTE_LANES] + p[:, ROUTE_LANES:] + jnp.dot(hlo, whi_ref[...], preferred_element_type=F32)
    lg = lg + b_ref[...]

    tm = lg.shape[0]
    lane = lax.broadcasted_iota(I32, (tm, ROUTE_LANES), 1)
    lane_f = lane.astype(F32)
    neg = jnp.float32(-1e30)
    big = jnp.float32(ROUTE_LANES)

    def first_max(vals):
        m = jnp.max(vals, axis=-1, keepdims=True)
        idx = jnp.min(jnp.where(vals == m, lane_f, big), axis=-1, keepdims=True)
        return m, idx

    gmask = lane < N_GROUPS
    mg, g_sel = first_max(jnp.where(gmask, lg, neg))
    gate_g = 1.0 / jnp.sum(jnp.where(gmask, jnp.exp(lg - mg), 0.0), axis=-1, keepdims=True)

    grp_of_lane = ((lane - EXP_LANE0) >> 3).astype(F32)
    emask = (lane >= EXP_LANE0) & (lane < EXP_LANE0 + N_EXPERTS) & (grp_of_lane == g_sel)
    le = jnp.where(emask, lg, neg)
    m1, i1 = first_max(le)
    m2, i2 = first_max(jnp.where(lane_f == i1, neg, le))
    e21 = jnp.exp(m2 - m1)
    w1 = 1.0 / (1.0 + e21)
    w2 = e21 * w1
    g1 = gate_g * w1
    g2 = gate_g * w2

    is1 = lane_f == i1
    is2 = lane_f == i2
    onehot = (is1 | is2).astype(F32)
    before = jnp.dot(tri_ref[...], onehot.astype(BF16), preferred_element_type=F32) + base_ref[0:1, :]
    r1 = jnp.sum(jnp.where(is1, before, 0.0), axis=-1, keepdims=True)
    r2 = jnp.sum(jnp.where(is2, before, 0.0), axis=-1, keepdims=True)
    new_base = base_ref[...] + jnp.sum(onehot, axis=0, keepdims=True)
    base_ref[...] = new_base
    cnt_ref[...] = new_base

    e1 = i1 - float(EXP_LANE0)
    e2 = i2 - float(EXP_LANE0)
    slab = jnp.where(lane == 0, e1, jnp.where(lane == 1, e2, jnp.where(lane == 2, g1, jnp.where(
        lane == 3, g2, jnp.where(lane == 4, r1, jnp.where(lane == 5, r2, 0.0))))))
    route_ref[...] = slab


def _router(x, g, wcat, whi, bias, tri, *, tm, name):
    t, d = x.shape
    assert t % tm == 0
    return pl.pallas_call(
        _router_body,
        grid=(t // tm,),
        in_specs=[
            pl.BlockSpec((tm, d), lambda i: (i, 0)),
            pl.BlockSpec((1, d), lambda i: (0, 0)),
            pl.BlockSpec((d, 2 * ROUTE_LANES), lambda i: (0, 0)),
            pl.BlockSpec((d, ROUTE_LANES), lambda i: (0, 0)),
            pl.BlockSpec((1, ROUTE_LANES), lambda i: (0, 0)),
            pl.BlockSpec((tm, tm), lambda i: (0, 0)),
        ],
        out_specs=[
            pl.BlockSpec((tm, d // 2), lambda i: (i, 0)),
            pl.BlockSpec((tm, ROUTE_LANES), lambda i: (i, 0)),
            pl.BlockSpec((SUBLANES, ROUTE_LANES), lambda i: (0, 0)),
        ],
        out_shape=[
            jax.ShapeDtypeStruct((t, d // 2), U32),
            jax.ShapeDtypeStruct((t, ROUTE_LANES), F32),
            jax.ShapeDtypeStruct((SUBLANES, ROUTE_LANES), F32),
        ],
        scratch_shapes=[pltpu.VMEM((SUBLANES, ROUTE_LANES), F32)],
        compiler_params=_cparams(("arbitrary",)),
        name=name,
    )(x, g, wcat, whi, bias, tri)


def _row_copies_wait(src_ref, dst_ref, sem):
    pltpu.make_async_copy(src_ref, dst_ref, sem).wait()


CAST_ROWS = 256


def _largest_divisor(n, candidates=(8, 4, 2, 1)):
    return next(c for c in candidates if n % c == 0)


def _stage_expert_weight(plan_ref, nb_ref, w_hbm, stage_ref, wb_ref, wsem, *, layer):
    i = pl.program_id(0)

    def w_copy(e, s):
        return pltpu.make_async_copy(w_hbm.at[layer, e], stage_ref.at[s], wsem.at[s])

    @pl.when(i == 0)
    def _():
        w_copy(plan_ref[0, 0], 0).start()

    @pl.when(jnp.logical_and(i < nb_ref[0], plan_ref[1, i] == 1))
    def _():
        s = plan_ref[2, i]
        w_copy(plan_ref[0, i], s).wait()

        @pl.when(plan_ref[4, i] == 1)
        def _():
            w_copy(plan_ref[3, i], 1 - s).start()

        def cast_rows(c, carry):
            r0 = pl.multiple_of(c * CAST_ROWS, CAST_ROWS)
            wb_ref[pl.ds(r0, CAST_ROWS), :] = stage_ref[s, pl.ds(r0, CAST_ROWS), :].astype(BF16)
            return carry

        lax.fori_loop(0, wb_ref.shape[0] // CAST_ROWS, cast_rows, 0)


def _dispatch_body(hp_ref, dest_hbm, xs_in_hbm, xs_hbm, idx_smem, rows_ref, isem, rsem, *, tm, idx_tiles, n_tiles):
    del xs_in_hbm
    i = pl.program_id(0)
    per_tile = 2 * tm

    @pl.when(i % idx_tiles == 0)
    def _():
        cp = pltpu.make_async_copy(dest_hbm.at[pl.ds((i // idx_tiles) * (idx_tiles * per_tile), idx_tiles * per_tile)],
                                   idx_smem, isem)
        cp.start()
        cp.wait()

    def wait_slot(s):
        for _ in range(2):
            _row_copies_wait(rows_ref.at[s], xs_hbm.at[pl.ds(0, tm)], rsem.at[s])

    for s in range(2):
        @pl.when(i % 2 == s)
        def _():
            @pl.when(i >= 2)
            def _():
                wait_slot(s)

            rows_ref[s] = hp_ref[...]
            base = (i % idx_tiles) * per_tile
            for t in range(tm):
                for k in range(2):
                    d = idx_smem[base + 2 * t + k]
                    pltpu.make_async_copy(rows_ref.at[s, pl.ds(t, 1)], xs_hbm.at[pl.ds(d, 1)],
                                          rsem.at[s]).start(priority=k)

    @pl.when(i == n_tiles - 1)
    def _():
        wait_slot((n_tiles - 1) % 2)
        if n_tiles >= 2:
            wait_slot(n_tiles % 2)


def _dispatch(hp, dest, xs_init, *, tm, name):
    t, dw = hp.shape
    assert t % tm == 0
    n_tiles = t // tm
    idx_tiles = _largest_divisor(n_tiles)
    return pl.pallas_call(
        functools.partial(_dispatch_body, tm=tm, idx_tiles=idx_tiles, n_tiles=n_tiles),
        grid=(n_tiles,),
        in_specs=[
            pl.BlockSpec((tm, dw), lambda i: (i, 0)),
            pl.BlockSpec(memory_space=pl.ANY),
            pl.BlockSpec(memory_space=pl.ANY),
        ],
        out_specs=pl.BlockSpec(memory_space=pl.ANY),
        out_shape=jax.ShapeDtypeStruct(xs_init.shape, xs_init.dtype),
        scratch_shapes=[
            pltpu.SMEM((idx_tiles * 2 * tm,), I32),
            pltpu.VMEM((2, tm, dw), U32),
            pltpu.SemaphoreType.DMA,
            pltpu.SemaphoreType.DMA((2,)),
        ],
        input_output_aliases={2: 0},
        compiler_params=_cparams(("arbitrary",)),
        name=name,
    )(hp, dest, xs_init)


def _expert_up_body(plan_ref, nb_ref, xs_ref, wu_hbm, hh_ref, stage_ref, wub_ref, wsem, *, layer):
    used = pl.program_id(0) < nb_ref[0]
    _stage_expert_weight(plan_ref, nb_ref, wu_hbm, stage_ref, wub_ref, wsem, layer=layer)

    @pl.when(jnp.logical_not(used))
    def _():
        hh_ref[...] = jnp.zeros_like(hh_ref)

    @pl.when(used)
    def _():
        w = xs_ref[...]
        half = w.shape[1]
        hidden = hh_ref.shape[1]
        hu = jnp.dot(_unpack_lo(w).astype(BF16), wub_ref[:half, :], preferred_element_type=F32)
        hu = hu + jnp.dot(_unpack_hi(w).astype(BF16), wub_ref[half:, :], preferred_element_type=F32)
        gate = hu[:, :hidden]
        hh_ref[...] = (gate * jax.nn.sigmoid(gate) * hu[:, hidden:]).astype(BF16)


def _expert_down_body(plan_ref, nb_ref, hh_ref, wd_hbm, ys_ref, stage_ref, wdb_ref, wsem, *, layer):
    _stage_expert_weight(plan_ref, nb_ref, wd_hbm, stage_ref, wdb_ref, wsem, layer=layer)
    used = pl.program_id(0) < nb_ref[0]

    @pl.when(jnp.logical_not(used))
    def _():
        ys_ref[...] = jnp.zeros_like(ys_ref)

    @pl.when(used)
    def _():
        ys_ref[...] = _pack_bf16_pairs(jnp.dot(hh_ref[...], wdb_ref[...], preferred_element_type=F32))


def _experts(plan, n_used, xs, w_up, w_down, *, layer, name):
    n_rows, dw = xs.shape
    n_blocks = n_rows // MOE_BM
    _, _, d, two_f = w_up.shape
    hidden = w_down.shape[2]
    any_spec = pl.BlockSpec(memory_space=pl.ANY)

    def row_map(i, plan_, nb):
        return (jnp.minimum(i, nb[0] - 1), 0)

    def out_map(i, plan_, nb):
        return (i, 0)

    hh = pl.pallas_call(
        functools.partial(_expert_up_body, layer=layer),
        grid_spec=pltpu.PrefetchScalarGridSpec(
            num_scalar_prefetch=2,
            grid=(n_blocks,),
            in_specs=[pl.BlockSpec((MOE_BM, dw), row_map), any_spec],
            out_specs=pl.BlockSpec((MOE_BM, hidden), out_map),
            scratch_shapes=[
                pltpu.VMEM((2, d, two_f), F32),
                pltpu.VMEM((d, two_f), BF16),
                pltpu.SemaphoreType.DMA((2,)),
            ],
        ),
        out_shape=jax.ShapeDtypeStruct((n_rows, hidden), BF16),
        compiler_params=_cparams(("arbitrary",)),
        name=f"{name}_up",
    )(plan, n_used, xs, w_up)
    return pl.pallas_call(
        functools.partial(_expert_down_body, layer=layer),
        grid_spec=pltpu.PrefetchScalarGridSpec(
            num_scalar_prefetch=2,
            grid=(n_blocks,),
            in_specs=[pl.BlockSpec((MOE_BM, hidden), row_map), any_spec],
            out_specs=pl.BlockSpec((MOE_BM, dw), out_map),
            scratch_shapes=[
                pltpu.VMEM((2, hidden, d), F32),
                pltpu.VMEM((hidden, d), BF16),
                pltpu.SemaphoreType.DMA((2,)),
            ],
        ),
        out_shape=jax.ShapeDtypeStruct((n_rows, dw), U32),
        compiler_params=_cparams(("arbitrary",)),
        name=f"{name}_down",
    )(plan, n_used, hh, w_down)


def _combine_body(x_ref, route_ref, dest_hbm, ys_hbm, o_ref, idx_smem, rows_ref, isem, rsem, *, tm, idx_tiles):
    i = pl.program_id(0)
    per_tile = 2 * tm

    def load_idx(group):
        cp = pltpu.make_async_copy(dest_hbm.at[pl.ds(group * (idx_tiles * per_tile), idx_tiles * per_tile)],
                                   idx_smem, isem)
        cp.start()
        cp.wait()

    def issue_rows(tile, s):
        base = (tile % idx_tiles) * per_tile
        for t in range(tm):
            for k in range(2):
                d = idx_smem[base + 2 * t + k]
                pltpu.make_async_copy(ys_hbm.at[pl.ds(d, 1)], rows_ref.at[s, k, pl.ds(t, 1)],
                                      rsem.at[s]).start(priority=k)

    @pl.when(i == 0)
    def _():
        load_idx(0)
        issue_rows(0, 0)

    nxt = i + 1

    @pl.when(nxt < pl.num_programs(0))
    def _():
        @pl.when(nxt % idx_tiles == 0)
        def _():
            load_idx(nxt // idx_tiles)

        for s in range(2):
            @pl.when(nxt % 2 == s)
            def _():
                issue_rows(nxt, s)

    s = i % 2
    for k in range(2):
        _row_copies_wait(ys_hbm.at[pl.ds(0, tm)], rows_ref.at[s, k], rsem.at[s])

    half = rows_ref.shape[3]
    g0 = route_ref[:, 2:3]
    g1 = route_ref[:, 3:4]
    w0 = rows_ref[s, 0]
    w1 = rows_ref[s, 1]
    o_ref[:, :half] = x_ref[:, :half] + (g0 * _unpack_lo(w0) + g1 * _unpack_lo(w1))
    o_ref[:, half:] = x_ref[:, half:] + (g0 * _unpack_hi(w0) + g1 * _unpack_hi(w1))


def _combine(x, route, dest, ys, *, tm, name):
    t, d = x.shape
    dw = ys.shape[1]
    assert t % tm == 0
    idx_tiles = _largest_divisor(t // tm)
    return pl.pallas_call(
        functools.partial(_combine_body, tm=tm, idx_tiles=idx_tiles),
        grid=(t // tm,),
        in_specs=[
            pl.BlockSpec((tm, d), lambda i: (i, 0)),
            pl.BlockSpec((tm, ROUTE_LANES), lambda i: (i, 0)),
            pl.BlockSpec(memory_space=pl.ANY),
            pl.BlockSpec(memory_space=pl.ANY),
        ],
        out_specs=pl.BlockSpec((tm, d), lambda i: (i, 0)),
        out_shape=jax.ShapeDtypeStruct((t, d), F32),
        scratch_shapes=[
            pltpu.SMEM((idx_tiles * 2 * tm,), I32),
            pltpu.VMEM((2, 2, tm, dw), U32),
            pltpu.SemaphoreType.DMA,
            pltpu.SemaphoreType.DMA((2,)),
        ],
        compiler_params=_cparams(("arbitrary",)),
        name=name,
    )(x, route, dest, ys)


def _final_norm_body(x_ref, g_ref, o_ref):
    o_ref[...] = _rms(x_ref[...], g_ref[...])


def _final_norm(x, g, *, row0, n_rows, tm, name):
    d = x.shape[1]
    tm = min(tm, n_rows)
    assert n_rows % tm == 0 and row0 % tm == 0
    blk0 = row0 // tm
    return pl.pallas_call(
        _final_norm_body,
        grid=(n_rows // tm,),
        in_specs=[pl.BlockSpec((tm, d), lambda i: (blk0 + i, 0)), pl.BlockSpec((1, d), lambda i: (0, 0))],
        out_specs=pl.BlockSpec((tm, d), lambda i: (i, 0)),
        out_shape=jax.ShapeDtypeStruct((n_rows, d), F32),
        compiler_params=_cparams(("parallel",)),
        name=name,
    )(x, g)


def _moe(x, g, wg, bg, we, be, w_up, w_down, *, layer):
    tag = str(layer)
    t, d = x.shape
    tm_route = min(512, t)
    tm_rows = min(256, t)

    wr = jnp.zeros((d, ROUTE_LANES), F32).at[:, :N_GROUPS].set(wg).at[:, EXP_LANE0:EXP_LANE0 + N_EXPERTS].set(we)
    wr_hi = wr.astype(BF16)
    wr_lo = (wr - wr_hi.astype(F32)).astype(BF16)
    bias = jnp.zeros((1, ROUTE_LANES), F32).at[0, :N_GROUPS].set(bg).at[0, EXP_LANE0:EXP_LANE0 + N_EXPERTS].set(be)
    ii = lax.broadcasted_iota(I32, (tm_route, tm_route), 0)
    jj = lax.broadcasted_iota(I32, (tm_route, tm_route), 1)
    tri = (jj < ii).astype(BF16)

    hp, route, cnt = _router(x, g, jnp.concatenate([wr_hi, wr_lo], axis=1), wr_hi, bias, tri,
                             tm=tm_route, name=f"router_{tag}")

    counts = cnt[0, EXP_LANE0:EXP_LANE0 + N_EXPERTS].astype(I32)
    padded = ((counts + MOE_BM - 1) // MOE_BM) * MOE_BM
    pad_end = jnp.cumsum(padded)
    pad_start = pad_end - padded
    e_ids = route[:, 0:2].astype(I32)
    rank = route[:, 4:6].astype(I32)
    e_onehot = e_ids[..., None] == jnp.arange(N_EXPERTS, dtype=I32)
    dest = (jnp.sum(jnp.where(e_onehot, pad_start, 0), axis=-1) + rank).reshape(-1)
    n_blocks = (2 * t) // MOE_BM + N_EXPERTS
    n_used = (pad_end[-1] // MOE_BM).astype(I32)
    blk = jnp.arange(n_blocks, dtype=I32)
    block_e = jnp.sum((pad_end[None, :] <= (jnp.minimum(blk, n_used - 1) * MOE_BM)[:, None]).astype(I32), axis=1)
    fresh = jnp.concatenate([jnp.ones((1,), I32), (block_e[1:] != block_e[:-1]).astype(I32)])
    slot = (jnp.cumsum(fresh) - 1) % 2
    next_blk = pad_end[block_e] // MOE_BM
    has_next = (next_blk < n_used).astype(I32)
    next_e = block_e[jnp.minimum(next_blk, n_used - 1)]
    plan = jnp.stack([block_e, fresh, slot, next_e, has_next]).astype(I32)

    xs = _dispatch(hp, dest, jnp.zeros((n_blocks * MOE_BM, d // 2), U32), tm=tm_rows, name=f"dispatch_{tag}")
    ys = _experts(plan, n_used.reshape(1), xs, w_up, w_down, layer=layer, name=f"experts_{tag}")
    return _combine(x, route, dest, ys, tm=tm_rows, name=f"combine_{tag}")


def kernel(x_prompt, x_sample, mem_prompt, mem_sample, g_mix, w_in, w_out, conv_w, conv_b, lru_wa, lru_ba, lru_wx, lru_bx, lru_lambda, sg_norm, sg_w, sg_b, g_xattn, g_mem, xa_wq, xa_wkv, xa_wo, g_moe, router_wg, router_bg, router_we, router_be, moe_w_up, moe_w_down, g_final):
    bp, seq, d = x_prompt.shape
    bs = x_sample.shape[0]
    assert x_sample.shape[1] == seq
    n_batch = bp + bs
    t = n_batch * seq
    n_mem = mem_prompt.shape[1]
    depth = w_in.shape[0]
    width = w_out.shape[1]
    n_heads = width // HEAD_W

    x = jnp.concatenate([x_prompt, x_sample], axis=0).reshape(t, d)
    mem = jnp.concatenate([mem_prompt, mem_sample], axis=0).reshape(n_batch * n_mem, d)

    for i in range(depth):
        j = i // 2
        w_in_i = w_in[i].astype(BF16)
        w_out_i = w_out[i].astype(BF16)
        if i % 2 == 0:
            z = _norm_matmul(x, g_mix[i][None], w_in_i, tm=512, tn=1024, gelu_cols=width, out_dtype=BF16,
                             name=f"mix_in_{i}")
            wcat = (0.5 * jnp.concatenate([lru_wa[j, 0], lru_wx[j, 0], lru_wa[j, 1], lru_wx[j, 1]], axis=-1)).astype(BF16)
            bcat = 0.5 * jnp.concatenate(
                [lru_ba[j, 0].reshape(n_heads, 1, HEAD_W), lru_bx[j, 0].reshape(n_heads, 1, HEAD_W),
                 lru_ba[j, 1].reshape(n_heads, 1, HEAD_W), lru_bx[j, 1].reshape(n_heads, 1, HEAD_W)], axis=-1)
            y = _lru_core(z, conv_w[j], conv_b[j][None], wcat, bcat, lru_lambda[j], n_batch=n_batch, seq=seq,
                          name=f"lru_{i}")
            x = _res_matmul(y, w_out_i, x, tm=512, tn=1024, name=f"mix_out_{i}")
        else:
            z = _norm_matmul(x, g_mix[i][None], w_in_i, tm=512, tn=1024, gelu_cols=2 * width, out_dtype=BF16,
                             name=f"mix_in_{i}")
            x = _gmlp_out(z, x, sg_norm[j][None], sg_w[j].astype(BF16), sg_b[j].T, w_out_i, tm=256,
                          name=f"gmlp_{i}")

        kv = _norm_matmul(mem, g_mem[i][None], xa_wkv[i].astype(BF16), tm=n_mem, tn=1024, gelu_cols=0,
                          out_dtype=BF16, name=f"mem_kv_{i}")
        x = _xattn(x, g_xattn[i][None], xa_wq[i].astype(BF16), kv, xa_wo[i].astype(BF16), seq=seq, n_mem=n_mem,
                   tm=512, name=f"xattn_{i}")
        x = _moe(x, g_moe[i][None], router_wg[i], router_bg[i], router_we[i], router_be[i],
                 moe_w_up, moe_w_down, layer=i)

    y_prompt = _final_norm(x, g_final[None], row0=0, n_rows=bp * seq, tm=512, name="final_prompt")
    y_sample = _final_norm(x, g_final[None], row0=bp * seq, n_rows=bs * seq, tm=512, name="final_sample")
    return (y_prompt.reshape(bp, seq, d), y_sample.reshape(bs, seq, d))
```

```python
import functools

import jax
import jax.numpy as jnp
from jax import lax
from jax.experimental import pallas as pl
from jax.experimental.pallas import tpu as pltpu

F32 = jnp.float32
BF16 = jnp.bfloat16
U32 = jnp.uint32
I32 = jnp.int32

EPS = 1e-6
LRU_C = 8.0
LOG2_E = 1.4426950408889634
TINY = 1e-30
HEAD_W = 128
CHUNK = 128
CONV_W = 4
XA_HEADS = 4
XA_HEAD_DIM = 128
N_GROUPS = 8
EXPERTS_PER_GROUP = 8
N_EXPERTS = 64
ROUTE_LANES = 128
EXP_LANE0 = N_GROUPS

VMEM_LIMIT_BYTES = 56 * 1024 * 1024
SUBLANES = 8

MOE_BM = 256
HI_MASK = 0xFFFF0000


def _cparams(semantics):
    return pltpu.CompilerParams(dimension_semantics=semantics, vmem_limit_bytes=VMEM_LIMIT_BYTES)


def _gelu(x):
    return 0.5 * x * (1.0 + jnp.tanh(0.7978845608028654 * (x + 0.044715 * (x * x * x))))


def _rms(x, g):
    ms = jnp.mean(x * x, axis=-1, keepdims=True)
    return x * lax.rsqrt(ms + EPS) * g


def _pack_bf16_pairs(x):
    n = x.shape[1] // 2
    bits = lax.bitcast_convert_type(x.astype(BF16).astype(F32), U32)
    return (bits[:, :n] >> 16) | (bits[:, n:] & jnp.uint32(HI_MASK))


def _unpack_lo(w):
    return lax.bitcast_convert_type(w << 16, F32)


def _unpack_hi(w):
    return lax.bitcast_convert_type(w & jnp.uint32(HI_MASK), F32)


def _norm_matmul_body(x_ref, g_ref, w_ref, o_ref, *, tn, n_gelu):
    h = _rms(x_ref[...], g_ref[...]).astype(BF16)
    for jc in range(w_ref.shape[1] // tn):
        cs = slice(jc * tn, (jc + 1) * tn)
        z = jnp.dot(h, w_ref[:, cs], preferred_element_type=F32)
        o_ref[:, cs] = (_gelu(z) if jc < n_gelu else z).astype(o_ref.dtype)


def _norm_matmul(x, g, w, *, tm, tn, gelu_cols, out_dtype, name):
    t, d = x.shape
    n = w.shape[1]
    tm = min(tm, t)
    tn = min(tn, n)
    assert t % tm == 0 and n % tn == 0 and gelu_cols % tn == 0
    return pl.pallas_call(
        functools.partial(_norm_matmul_body, tn=tn, n_gelu=gelu_cols // tn),
        grid=(t // tm,),
        in_specs=[
            pl.BlockSpec((tm, d), lambda i: (i, 0)),
            pl.BlockSpec((1, d), lambda i: (0, 0)),
            pl.BlockSpec((d, n), lambda i: (0, 0)),
        ],
        out_specs=pl.BlockSpec((tm, n), lambda i: (i, 0)),
        out_shape=jax.ShapeDtypeStruct((t, n), out_dtype),
        compiler_params=_cparams(("parallel",)),
        name=name,
    )(x, g, w)


def _res_matmul_body(y_ref, w_ref, x_ref, o_ref, *, tn):
    y = y_ref[...]
    for jc in range(w_ref.shape[1] // tn):
        cs = slice(jc * tn, (jc + 1) * tn)
        o_ref[:, cs] = x_ref[:, cs] + jnp.dot(y, w_ref[:, cs], preferred_element_type=F32)


def _res_matmul(y, w, x, *, tm, tn, name):
    t, k = y.shape
    n = w.shape[1]
    tm = min(tm, t)
    assert t % tm == 0 and n % tn == 0
    return pl.pallas_call(
        functools.partial(_res_matmul_body, tn=tn),
        grid=(t // tm,),
        in_specs=[
            pl.BlockSpec((tm, k), lambda i: (i, 0)),
            pl.BlockSpec((k, n), lambda i: (0, 0)),
            pl.BlockSpec((tm, n), lambda i: (i, 0)),
        ],
        out_specs=pl.BlockSpec((tm, n), lambda i: (i, 0)),
        out_shape=jax.ShapeDtypeStruct((t, n), F32),
        compiler_params=_cparams(("parallel",)),
        name=name,
    )(y, w, x)


LRU_CW = 256
LRU_ROWS = 256
LRU_PAD = 8


def _lru_body(zg_ref, zx_ref, cw_ref, cb_ref, w_ref, b_ref, lam_ref, o_ref,
              xf_ref, af_ref, bf_ref, ab_ref, bb_ref, *, seq):
    cw = LRU_CW
    rows = LRU_ROWS
    zero_pad = jnp.zeros((LRU_PAD, cw), F32)
    xf_ref[0:LRU_PAD, :] = zero_pad
    xf_ref[LRU_PAD + seq:LRU_PAD + seq + LRU_PAD, :] = zero_pad
    xf_ref[LRU_PAD:LRU_PAD + seq, :] = zx_ref[...].astype(F32)

    nl = -lam_ref[...]
    softplus = jnp.maximum(nl, 0.0) + jnp.log1p(jnp.exp(-jnp.abs(nl)))
    half_decay = (-0.5 * LRU_C * LOG2_E) * softplus
    cwv = cw_ref[...]
    cbv = cb_ref[...]
    dirs = ((af_ref, bf_ref), (ab_ref, bb_ref))

    def gate_chunk(c, carry):
        t0 = pl.multiple_of(c * rows, rows)
        xw = xf_ref[pl.ds(t0, rows + 2 * LRU_PAD), :]
        xc = cbv
        for k in range(CONV_W):
            off = LRU_PAD + k - CONV_W // 2
            xc = xc + xw[off:off + rows, :] * cwv[k:k + 1, :]
        for hh in range(cw // HEAD_W):
            sl = slice(hh * HEAD_W, (hh + 1) * HEAD_W)
            xch = xc[:, sl]
            th = jnp.tanh(jnp.dot(xch.astype(BF16), w_ref[hh], preferred_element_type=F32) + b_ref[hh])
            half_x = 0.5 * xch
            for d, (a_ref, b_ref_d) in enumerate(dirs):
                th_r = th[:, (2 * d) * HEAD_W:(2 * d + 1) * HEAD_W]
                th_i = th[:, (2 * d + 1) * HEAD_W:(2 * d + 2) * HEAD_W]
                hd = half_decay[d:d + 1, sl]
                a = jnp.exp2(th_r * hd + hd)
                s = 1.0 - a * a
                a_ref[pl.ds(t0, rows), sl] = a
                b_ref_d[pl.ds(t0, rows), sl] = (s * lax.rsqrt(jnp.maximum(s, TINY))) * ((th_i + 1.0) * half_x)
        return carry

    lax.fori_loop(0, seq // rows, gate_chunk, 0)

    row = lax.broadcasted_iota(I32, (SUBLANES, cw), 0)
    n_tiles = seq // SUBLANES

    def scan_tile(k, carry):
        cf, cb = carry
        tf = pl.multiple_of(k * SUBLANES, SUBLANES)
        a = af_ref[pl.ds(tf, SUBLANES), :]
        b = bf_ref[pl.ds(tf, SUBLANES), :]
        for s in (1, 2, 4):
            m = row >= s
            b = jnp.where(m, a * pltpu.roll(b, s, 0) + b, b)
            a = jnp.where(m, a * pltpu.roll(a, s, 0), a)
        h = a * cf + b
        bf_ref[pl.ds(tf, SUBLANES), :] = h
        cf = jnp.broadcast_to(h[SUBLANES - 1:SUBLANES, :], (SUBLANES, cw))
        tb = pl.multiple_of((n_tiles - 1 - k) * SUBLANES, SUBLANES)
        a = ab_ref[pl.ds(tb, SUBLANES), :]
        b = bb_ref[pl.ds(tb, SUBLANES), :]
        for s in (1, 2, 4):
            m = row < SUBLANES - s
            b = jnp.where(m, a * pltpu.roll(b, SUBLANES - s, 0) + b, b)
            a = jnp.where(m, a * pltpu.roll(a, SUBLANES - s, 0), a)
        h = a * cb + b
        bb_ref[pl.ds(tb, SUBLANES), :] = h
        cb = jnp.broadcast_to(h[0:1, :], (SUBLANES, cw))
        return cf, cb

    zero_tile = jnp.zeros((SUBLANES, cw), F32)
    lax.fori_loop(0, n_tiles, scan_tile, (zero_tile, zero_tile), unroll=8)

    def out_chunk(c, carry):
        t0 = pl.multiple_of(c * rows, rows)
        y = zg_ref[pl.ds(t0, rows), :].astype(F32) * (bf_ref[pl.ds(t0, rows), :] + bb_ref[pl.ds(t0, rows), :])
        o_ref[pl.ds(t0, rows), :] = y.astype(BF16)
        return carry

    lax.fori_loop(0, seq // rows, out_chunk, 0)


def _lru_core(z, conv_w, conv_b, wcat, bcat, lam, *, n_batch, seq, name):
    t, two_w = z.shape
    width = two_w // 2
    cw = LRU_CW
    n_ct = width // cw
    hpt = cw // HEAD_W
    assert seq % LRU_ROWS == 0 and width % cw == 0
    return pl.pallas_call(
        functools.partial(_lru_body, seq=seq),
        grid=(n_batch, n_ct),
        in_specs=[
            pl.BlockSpec((seq, cw), lambda b, c: (b, c)),
            pl.BlockSpec((seq, cw), lambda b, c: (b, n_ct + c)),
            pl.BlockSpec((CONV_W, cw), lambda b, c: (0, c)),
            pl.BlockSpec((1, cw), lambda b, c: (0, c)),
            pl.BlockSpec((hpt, HEAD_W, 4 * HEAD_W), lambda b, c: (c, 0, 0)),
            pl.BlockSpec((hpt, 1, 4 * HEAD_W), lambda b, c: (c, 0, 0)),
            pl.BlockSpec((2, cw), lambda b, c: (0, c)),
        ],
        out_specs=pl.BlockSpec((seq, cw), lambda b, c: (b, c)),
        out_shape=jax.ShapeDtypeStruct((t, width), BF16),
        scratch_shapes=[
            pltpu.VMEM((seq + 2 * LRU_PAD, cw), F32),
            pltpu.VMEM((seq, cw), F32),
            pltpu.VMEM((seq, cw), F32),
            pltpu.VMEM((seq, cw), F32),
            pltpu.VMEM((seq, cw), F32),
        ],
        compiler_params=_cparams(("parallel", "parallel")),
        name=name,
    )(z, z, conv_w, conv_b, wcat, bcat, lam)


def _gmlp_body(z_ref, x_ref, sgn_ref, sgw_ref, sgbt_ref, w_ref, o_ref, vn_ref, y_ref, *, tm, width):
    v = z_ref[:, width:].astype(F32)
    vn_ref[...] = _rms(v, sgn_ref[...]).astype(BF16)
    for n in range(tm // CHUNK):
        rs = slice(n * CHUNK, (n + 1) * CHUNK)
        for h in range(width // HEAD_W):
            cs = slice(h * HEAD_W, (h + 1) * HEAD_W)
            mixed = jnp.dot(sgw_ref[h], vn_ref[rs, cs], preferred_element_type=F32) + sgbt_ref[:, h:h + 1]
            y_ref[rs, cs] = (z_ref[rs, cs].astype(F32) * mixed).astype(BF16)
    o_ref[...] = x_ref[...] + jnp.dot(y_ref[...], w_ref[...], preferred_element_type=F32)


def _gmlp_out(z, x, sg_norm, sg_w, sg_bt, w_out, *, tm, name):
    t, two_w = z.shape
    width = two_w // 2
    d = w_out.shape[1]
    n_heads = width // HEAD_W
    tm = min(tm, t)
    assert t % tm == 0 and tm % CHUNK == 0
    return pl.pallas_call(
        functools.partial(_gmlp_body, tm=tm, width=width),
        grid=(t // tm,),
        in_specs=[
            pl.BlockSpec((tm, two_w), lambda i: (i, 0)),
            pl.BlockSpec((tm, d), lambda i: (i, 0)),
            pl.BlockSpec((1, width), lambda i: (0, 0)),
            pl.BlockSpec((n_heads, CHUNK, CHUNK), lambda i: (0, 0, 0)),
            pl.BlockSpec((CHUNK, n_heads), lambda i: (0, 0)),
            pl.BlockSpec((width, d), lambda i: (0, 0)),
        ],
        out_specs=pl.BlockSpec((tm, d), lambda i: (i, 0)),
        out_shape=jax.ShapeDtypeStruct((t, d), F32),
        scratch_shapes=[pltpu.VMEM((tm, width), BF16), pltpu.VMEM((tm, width), BF16)],
        compiler_params=_cparams(("parallel",)),
        name=name,
    )(z, x, sg_norm, sg_w, sg_bt, w_out)


def _xattn_body(x_ref, g_ref, wq_ref, kv_ref, wo_ref, o_ref):
    x = x_ref[...]
    h = _rms(x, g_ref[...]).astype(BF16)
    q = jnp.dot(h, wq_ref[...], preferred_element_type=F32) * (XA_HEAD_DIM ** -0.5)
    qb = q.astype(BF16)
    xa_w = XA_HEADS * XA_HEAD_DIM
    outs = []
    for hd in range(XA_HEADS):
        cs = slice(hd * XA_HEAD_DIM, (hd + 1) * XA_HEAD_DIM)
        vs = slice(xa_w + hd * XA_HEAD_DIM, xa_w + (hd + 1) * XA_HEAD_DIM)
        s = lax.dot_general(qb[:, cs], kv_ref[:, cs], (((1,), (1,)), ((), ())), preferred_element_type=F32)
        p = jnp.exp(s - jnp.max(s, axis=-1, keepdims=True))
        l = jnp.sum(p, axis=-1, keepdims=True)
        o = jnp.dot(p.astype(BF16), kv_ref[:, vs], preferred_element_type=F32) / l
        outs.append(o.astype(BF16))
    o = jnp.concatenate(outs, axis=1)
    o_ref[...] = x + jnp.dot(o, wo_ref[...], preferred_element_type=F32)


def _xattn(x, g, wq, kv, wo, *, seq, n_mem, tm, name):
    t, d = x.shape
    xa_w = wq.shape[1]
    tm = min(tm, seq)
    assert seq % tm == 0
    tiles_per_seq = seq // tm
    return pl.pallas_call(
        _xattn_body,
        grid=(t // tm,),
        in_specs=[
            pl.BlockSpec((tm, d), lambda i: (i, 0)),
            pl.BlockSpec((1, d), lambda i: (0, 0)),
            pl.BlockSpec((d, xa_w), lambda i: (0, 0)),
            pl.BlockSpec((n_mem, 2 * xa_w), lambda i: (i // tiles_per_seq, 0)),
            pl.BlockSpec((xa_w, d), lambda i: (0, 0)),
        ],
        out_specs=pl.BlockSpec((tm, d), lambda i: (i, 0)),
        out_shape=jax.ShapeDtypeStruct((t, d), F32),
        compiler_params=_cparams(("parallel",)),
        name=name,
    )(x, g, wq, kv, wo)


def _router_body(x_ref, g_ref, wcat_ref, whi_ref, b_ref, tri_ref, hp_ref, route_ref, cnt_ref, base_ref):
    @pl.when(pl.program_id(0) == 0)
    def _():
        base_ref[...] = jnp.zeros_like(base_ref)

    h32 = _rms(x_ref[...], g_ref[...])
    hhi = h32.astype(BF16)
    hlo = (h32 - hhi.astype(F32)).astype(BF16)
    hp_ref[...] = _pack_bf16_pairs(h32)

    p = jnp.dot(hhi, wcat_ref[...], preferred_element_type=F32)
    lg = p[:, :ROUTE_LANES] + p[:, ROUTE_LANES:] + jnp.dot(hlo, whi_ref[...], preferred_element_type=F32)
    lg = lg + b_ref[...]

    tm = lg.shape[0]
    lane = lax.broadcasted_iota(I32, (tm, ROUTE_LANES), 1)
    lane_f = lane.astype(F32)
    neg = jnp.float32(-1e30)
    big = jnp.float32(ROUTE_LANES)

    def first_max(vals):
        m = jnp.max(vals, axis=-1, keepdims=True)
        idx = jnp.min(jnp.where(vals == m, lane_f, big), axis=-1, keepdims=True)
        return m, idx

    gmask = lane < N_GROUPS
    mg, g_sel = first_max(jnp.where(gmask, lg, neg))
    gate_g = 1.0 / jnp.sum(jnp.where(gmask, jnp.exp(lg - mg), 0.0), axis=-1, keepdims=True)

    grp_of_lane = ((lane - EXP_LANE0) >> 3).astype(F32)
    emask = (lane >= EXP_LANE0) & (lane < EXP_LANE0 + N_EXPERTS) & (grp_of_lane == g_sel)
    le = jnp.where(emask, lg, neg)
    m1, i1 = first_max(le)
    m2, i2 = first_max(jnp.where(lane_f == i1, neg, le))
    e21 = jnp.exp(m2 - m1)
    w1 = 1.0 / (1.0 + e21)
    w2 = e21 * w1
    g1 = gate_g * w1
    g2 = gate_g * w2

    is1 = lane_f == i1
    is2 = lane_f == i2
    onehot = (is1 | is2).astype(F32)
    before = jnp.dot(tri_ref[...], onehot.astype(BF16), preferred_element_type=F32) + base_ref[0:1, :]
    r1 = jnp.sum(jnp.where(is1, before, 0.0), axis=-1, keepdims=True)
    r2 = jnp.sum(jnp.where(is2, before, 0.0), axis=-1, keepdims=True)
    new_base = base_ref[...] + jnp.sum(onehot, axis=0, keepdims=True)
    base_ref[...] = new_base
    cnt_ref[...] = new_base

    e1 = i1 - float(EXP_LANE0)
    e2 = i2 - float(EXP_LANE0)
    slab = jnp.where(lane == 0, e1, jnp.where(lane == 1, e2, jnp.where(lane == 2, g1, jnp.where(
        lane == 3, g2, jnp.where(lane == 4, r1, jnp.where(lane == 5, r2, 0.0))))))
    route_ref[...] = slab


def _router(x, g, wcat, whi, bias, tri, *, tm, name):
    t, d = x.shape
    assert t % tm == 0
    return pl.pallas_call(
        _router_body,
        grid=(t // tm,),
        in_specs=[
            pl.BlockSpec((tm, d), lambda i: (i, 0)),
            pl.BlockSpec((1, d), lambda i: (0, 0)),
            pl.BlockSpec((d, 2 * ROUTE_LANES), lambda i: (0, 0)),
            pl.BlockSpec((d, ROUTE_LANES), lambda i: (0, 0)),
            pl.BlockSpec((1, ROUTE_LANES), lambda i: (0, 0)),
            pl.BlockSpec((tm, tm), lambda i: (0, 0)),
        ],
        out_specs=[
            pl.BlockSpec((tm, d // 2), lambda i: (i, 0)),
            pl.BlockSpec((tm, ROUTE_LANES), lambda i: (i, 0)),
            pl.BlockSpec((SUBLANES, ROUTE_LANES), lambda i: (0, 0)),
        ],
        out_shape=[
            jax.ShapeDtypeStruct((t, d // 2), U32),
            jax.ShapeDtypeStruct((t, ROUTE_LANES), F32),
            jax.ShapeDtypeStruct((SUBLANES, ROUTE_LANES), F32),
        ],
        scratch_shapes=[pltpu.VMEM((SUBLANES, ROUTE_LANES), F32)],
        compiler_params=_cparams(("arbitrary",)),
        name=name,
    )(x, g, wcat, whi, bias, tri)


def _row_copies_wait(src_ref, dst_ref, sem):
    pltpu.make_async_copy(src_ref, dst_ref, sem).wait()


CAST_ROWS = 256


def _largest_divisor(n, candidates=(8, 4, 2, 1)):
    return next(c for c in candidates if n % c == 0)


def _stage_expert_weight(plan_ref, nb_ref, w_hbm, stage_ref, wb_ref, wsem, *, layer):
    i = pl.program_id(0)

    def w_copy(e, s):
        return pltpu.make_async_copy(w_hbm.at[layer, e], stage_ref.at[s], wsem.at[s])

    @pl.when(i == 0)
    def _():
        w_copy(plan_ref[0, 0], 0).start()

    @pl.when(jnp.logical_and(i < nb_ref[0], plan_ref[1, i] == 1))
    def _():
        s = plan_ref[2, i]
        w_copy(plan_ref[0, i], s).wait()

        @pl.when(plan_ref[4, i] == 1)
        def _():
            w_copy(plan_ref[3, i], 1 - s).start()

        def cast_rows(c, carry):
            r0 = pl.multiple_of(c * CAST_ROWS, CAST_ROWS)
            wb_ref[pl.ds(r0, CAST_ROWS), :] = stage_ref[s, pl.ds(r0, CAST_ROWS), :].astype(BF16)
            return carry

        lax.fori_loop(0, wb_ref.shape[0] // CAST_ROWS, cast_rows, 0)


def _dispatch_body(hp_ref, dest_hbm, xs_in_hbm, xs_hbm, idx_smem, rows_ref, isem, rsem, *, tm, idx_tiles, n_tiles):
    del xs_in_hbm
    i = pl.program_id(0)
    per_tile = 2 * tm

    @pl.when(i % idx_tiles == 0)
    def _():
        cp = pltpu.make_async_copy(dest_hbm.at[pl.ds((i // idx_tiles) * (idx_tiles * per_tile), idx_tiles * per_tile)],
                                   idx_smem, isem)
        cp.start()
        cp.wait()

    def wait_slot(s):
        for _ in range(2):
            _row_copies_wait(rows_ref.at[s], xs_hbm.at[pl.ds(0, tm)], rsem.at[s])

    for s in range(2):
        @pl.when(i % 2 == s)
        def _():
            @pl.when(i >= 2)
            def _():
                wait_slot(s)

            rows_ref[s] = hp_ref[...]
            base = (i % idx_tiles) * per_tile
            for t in range(tm):
                for k in range(2):
                    d = idx_smem[base + 2 * t + k]
                    pltpu.make_async_copy(rows_ref.at[s, pl.ds(t, 1)], xs_hbm.at[pl.ds(d, 1)],
                                          rsem.at[s]).start(priority=k)

    @pl.when(i == n_tiles - 1)
    def _():
        wait_slot((n_tiles - 1) % 2)
        if n_tiles >= 2:
            wait_slot(n_tiles % 2)


def _dispatch(hp, dest, xs_init, *, tm, name):
    t, dw = hp.shape
    assert t % tm == 0
    n_tiles = t // tm
    idx_tiles = _largest_divisor(n_tiles)
    return pl.pallas_call(
        functools.partial(_dispatch_body, tm=tm, idx_tiles=idx_tiles, n_tiles=n_tiles),
        grid=(n_tiles,),
        in_specs=[
            pl.BlockSpec((tm, dw), lambda i: (i, 0)),
            pl.BlockSpec(memory_space=pl.ANY),
            pl.BlockSpec(memory_space=pl.ANY),
        ],
        out_specs=pl.BlockSpec(memory_space=pl.ANY),
        out_shape=jax.ShapeDtypeStruct(xs_init.shape, xs_init.dtype),
        scratch_shapes=[
            pltpu.SMEM((idx_tiles * 2 * tm,), I32),
            pltpu.VMEM((2, tm, dw), U32),
            pltpu.SemaphoreType.DMA,
            pltpu.SemaphoreType.DMA((2,)),
        ],
        input_output_aliases={2: 0},
        compiler_params=_cparams(("arbitrary",)),
        name=name,
    )(hp, dest, xs_init)


def _expert_up_body(plan_ref, nb_ref, xs_ref, wu_hbm, hh_ref, stage_ref, wub_ref, wsem, *, layer):
    used = pl.program_id(0) < nb_ref[0]
    _stage_expert_weight(plan_ref, nb_ref, wu_hbm, stage_ref, wub_ref, wsem, layer=layer)

    @pl.when(jnp.logical_not(used))
    def _():
        hh_ref[...] = jnp.zeros_like(hh_ref)

    @pl.when(used)
    def _():
        w = xs_ref[...]
        half = w.shape[1]
        hidden = hh_ref.shape[1]
        hu = jnp.dot(_unpack_lo(w).astype(BF16), wub_ref[:half, :], preferred_element_type=F32)
        hu = hu + jnp.dot(_unpack_hi(w).astype(BF16), wub_ref[half:, :], preferred_element_type=F32)
        gate = hu[:, :hidden]
        hh_ref[...] = (gate * jax.nn.sigmoid(gate) * hu[:, hidden:]).astype(BF16)


def _expert_down_body(plan_ref, nb_ref, hh_ref, wd_hbm, ys_ref, stage_ref, wdb_ref, wsem, *, layer):
    _stage_expert_weight(plan_ref, nb_ref, wd_hbm, stage_ref, wdb_ref, wsem, layer=layer)
    used = pl.program_id(0) < nb_ref[0]

    @pl.when(jnp.logical_not(used))
    def _():
        ys_ref[...] = jnp.zeros_like(ys_ref)

    @pl.when(used)
    def _():
        ys_ref[...] = _pack_bf16_pairs(jnp.dot(hh_ref[...], wdb_ref[...], preferred_element_type=F32))


def _experts(plan, n_used, xs, w_up, w_down, *, layer, name):
    n_rows, dw = xs.shape
    n_blocks = n_rows // MOE_BM
    _, _, d, two_f = w_up.shape
    hidden = w_down.shape[2]
    any_spec = pl.BlockSpec(memory_space=pl.ANY)

    def row_map(i, plan_, nb):
        return (jnp.minimum(i, nb[0] - 1), 0)

    def out_map(i, plan_, nb):
        return (i, 0)

    hh = pl.pallas_call(
        functools.partial(_expert_up_body, layer=layer),
        grid_spec=pltpu.PrefetchScalarGridSpec(
            num_scalar_prefetch=2,
            grid=(n_blocks,),
            in_specs=[pl.BlockSpec((MOE_BM, dw), row_map), any_spec],
            out_specs=pl.BlockSpec((MOE_BM, hidden), out_map),
            scratch_shapes=[
                pltpu.VMEM((2, d, two_f), F32),
                pltpu.VMEM((d, two_f), BF16),
                pltpu.SemaphoreType.DMA((2,)),
            ],
        ),
        out_shape=jax.ShapeDtypeStruct((n_rows, hidden), BF16),
        compiler_params=_cparams(("arbitrary",)),
        name=f"{name}_up",
    )(plan, n_used, xs, w_up)
    return pl.pallas_call(
        functools.partial(_expert_down_body, layer=layer),
        grid_spec=pltpu.PrefetchScalarGridSpec(
            num_scalar_prefetch=2,
            grid=(n_blocks,),
            in_specs=[pl.BlockSpec((MOE_BM, hidden), row_map), any_spec],
            out_specs=pl.BlockSpec((MOE_BM, dw), out_map),
            scratch_shapes=[
                pltpu.VMEM((2, hidden, d), F32),
                pltpu.VMEM((hidden, d), BF16),
                pltpu.SemaphoreType.DMA((2,)),
            ],
        ),
        out_shape=jax.ShapeDtypeStruct((n_rows, dw), U32),
        compiler_params=_cparams(("arbitrary",)),
        name=f"{name}_down",
    )(plan, n_used, hh, w_down)


def _combine_body(x_ref, route_ref, dest_hbm, ys_hbm, o_ref, idx_smem, rows_ref, isem, rsem, *, tm, idx_tiles):
    i = pl.program_id(0)
    per_tile = 2 * tm

    def load_idx(group):
        cp = pltpu.make_async_copy(dest_hbm.at[pl.ds(group * (idx_tiles * per_tile), idx_tiles * per_tile)],
                                   idx_smem, isem)
        cp.start()
        cp.wait()

    def issue_rows(tile, s):
        base = (tile % idx_tiles) * per_tile
        for t in range(tm):
            for k in range(2):
                d = idx_smem[base + 2 * t + k]
                pltpu.make_async_copy(ys_hbm.at[pl.ds(d, 1)], rows_ref.at[s, k, pl.ds(t, 1)],
                                      rsem.at[s]).start(priority=k)

    @pl.when(i == 0)
    def _():
        load_idx(0)
        issue_rows(0, 0)

    nxt = i + 1

    @pl.when(nxt < pl.num_programs(0))
    def _():
        @pl.when(nxt % idx_tiles == 0)
        def _():
            load_idx(nxt // idx_tiles)

        for s in range(2):
            @pl.when(nxt % 2 == s)
            def _():
                issue_rows(nxt, s)

    s = i % 2
    for k in range(2):
        _row_copies_wait(ys_hbm.at[pl.ds(0, tm)], rows_ref.at[s, k], rsem.at[s])

    half = rows_ref.shape[3]
    g0 = route_ref[:, 2:3]
    g1 = route_ref[:, 3:4]
    w0 = rows_ref[s, 0]
    w1 = rows_ref[s, 1]
    o_ref[:, :half] = x_ref[:, :half] + (g0 * _unpack_lo(w0) + g1 * _unpack_lo(w1))
    o_ref[:, half:] = x_ref[:, half:] + (g0 * _unpack_hi(w0) + g1 * _unpack_hi(w1))


def _combine(x, route, dest, ys, *, tm, name):
    t, d = x.shape
    dw = ys.shape[1]
    assert t % tm == 0
    idx_tiles = _largest_divisor(t // tm)
    return pl.pallas_call(
        functools.partial(_combine_body, tm=tm, idx_tiles=idx_tiles),
        grid=(t // tm,),
        in_specs=[
            pl.BlockSpec((tm, d), lambda i: (i, 0)),
            pl.BlockSpec((tm, ROUTE_LANES), lambda i: (i, 0)),
            pl.BlockSpec(memory_space=pl.ANY),
            pl.BlockSpec(memory_space=pl.ANY),
        ],
        out_specs=pl.BlockSpec((tm, d), lambda i: (i, 0)),
        out_shape=jax.ShapeDtypeStruct((t, d), F32),
        scratch_shapes=[
            pltpu.SMEM((idx_tiles * 2 * tm,), I32),
            pltpu.VMEM((2, 2, tm, dw), U32),
            pltpu.SemaphoreType.DMA,
            pltpu.SemaphoreType.DMA((2,)),
        ],
        compiler_params=_cparams(("arbitrary",)),
        name=name,
    )(x, route, dest, ys)


def _final_norm_body(x_ref, g_ref, o_ref):
    o_ref[...] = _rms(x_ref[...], g_ref[...])


def _final_norm(x, g, *, row0, n_rows, tm, name):
    d = x.shape[1]
    tm = min(tm, n_rows)
    assert n_rows % tm == 0 and row0 % tm == 0
    blk0 = row0 // tm
    return pl.pallas_call(
        _final_norm_body,
        grid=(n_rows // tm,),
        in_specs=[pl.BlockSpec((tm, d), lambda i: (blk0 + i, 0)), pl.BlockSpec((1, d), lambda i: (0, 0))],
        out_specs=pl.BlockSpec((tm, d), lambda i: (i, 0)),
        out_shape=jax.ShapeDtypeStruct((n_rows, d), F32),
        compiler_params=_cparams(("parallel",)),
        name=name,
    )(x, g)


def _moe(x, g, wg, bg, we, be, w_up, w_down, xs_init, *, layer):
    tag = str(layer)
    t, d = x.shape
    tm_route = min(512, t)
    tm_rows = min(256, t)

    wr = jnp.zeros((d, ROUTE_LANES), F32).at[:, :N_GROUPS].set(wg).at[:, EXP_LANE0:EXP_LANE0 + N_EXPERTS].set(we)
    wr_hi = wr.astype(BF16)
    wr_lo = (wr - wr_hi.astype(F32)).astype(BF16)
    bias = jnp.zeros((1, ROUTE_LANES), F32).at[0, :N_GROUPS].set(bg).at[0, EXP_LANE0:EXP_LANE0 + N_EXPERTS].set(be)
    ii = lax.broadcasted_iota(I32, (tm_route, tm_route), 0)
    jj = lax.broadcasted_iota(I32, (tm_route, tm_route), 1)
    tri = (jj < ii).astype(BF16)

    hp, route, cnt = _router(x, g, jnp.concatenate([wr_hi, wr_lo], axis=1), wr_hi, bias, tri,
                             tm=tm_route, name=f"router_{tag}")

    counts = cnt[0, EXP_LANE0:EXP_LANE0 + N_EXPERTS].astype(I32)
    padded = ((counts + MOE_BM - 1) // MOE_BM) * MOE_BM
    pad_end = jnp.cumsum(padded)
    pad_start = pad_end - padded
    e_ids = route[:, 0:2].astype(I32)
    rank = route[:, 4:6].astype(I32)
    e_onehot = e_ids[..., None] == jnp.arange(N_EXPERTS, dtype=I32)
    dest = (jnp.sum(jnp.where(e_onehot, pad_start, 0), axis=-1) + rank).reshape(-1)
    n_blocks = (2 * t) // MOE_BM + N_EXPERTS
    n_used = (pad_end[-1] // MOE_BM).astype(I32)
    blk = jnp.arange(n_blocks, dtype=I32)
    block_e = jnp.sum((pad_end[None, :] <= (jnp.minimum(blk, n_used - 1) * MOE_BM)[:, None]).astype(I32), axis=1)
    fresh = jnp.concatenate([jnp.ones((1,), I32), (block_e[1:] != block_e[:-1]).astype(I32)])
    slot = (jnp.cumsum(fresh) - 1) % 2
    next_blk = pad_end[block_e] // MOE_BM
    has_next = (next_blk < n_used).astype(I32)
    next_e = block_e[jnp.minimum(next_blk, n_used - 1)]
    plan = jnp.stack([block_e, fresh, slot, next_e, has_next]).astype(I32)

    if xs_init is None:
        xs_init = jnp.zeros((n_blocks * MOE_BM, d // 2), U32)
    xs = _dispatch(hp, dest, xs_init, tm=tm_rows, name=f"dispatch_{tag}")
    ys = _experts(plan, n_used.reshape(1), xs, w_up, w_down, layer=layer, name=f"experts_{tag}")
    return _combine(x, route, dest, ys, tm=tm_rows, name=f"combine_{tag}"), ys


def kernel(x_prompt, x_sample, mem_prompt, mem_sample, g_mix, w_in, w_out, conv_w, conv_b, lru_wa, lru_ba, lru_wx, lru_bx, lru_lambda, sg_norm, sg_w, sg_b, g_xattn, g_mem, xa_wq, xa_wkv, xa_wo, g_moe, router_wg, router_bg, router_we, router_be, moe_w_up, moe_w_down, g_final):
    bp, seq, d = x_prompt.shape
    bs = x_sample.shape[0]
    assert x_sample.shape[1] == seq
    n_batch = bp + bs
    t = n_batch * seq
    n_mem = mem_prompt.shape[1]
    depth = w_in.shape[0]
    width = w_out.shape[1]
    n_heads = width // HEAD_W

    x = jnp.concatenate([x_prompt, x_sample], axis=0).reshape(t, d)
    mem = jnp.concatenate([mem_prompt, mem_sample], axis=0).reshape(n_batch * n_mem, d)

    ys_prev = None
    for i in range(depth):
        j = i // 2
        w_in_i = w_in[i].astype(BF16)
        w_out_i = w_out[i].astype(BF16)
        if i % 2 == 0:
            z = _norm_matmul(x, g_mix[i][None], w_in_i, tm=512, tn=1024, gelu_cols=width, out_dtype=BF16,
                             name=f"mix_in_{i}")
            wcat = (0.5 * jnp.concatenate([lru_wa[j, 0], lru_wx[j, 0], lru_wa[j, 1], lru_wx[j, 1]], axis=-1)).astype(BF16)
            bcat = 0.5 * jnp.concatenate(
                [lru_ba[j, 0].reshape(n_heads, 1, HEAD_W), lru_bx[j, 0].reshape(n_heads, 1, HEAD_W),
                 lru_ba[j, 1].reshape(n_heads, 1, HEAD_W), lru_bx[j, 1].reshape(n_heads, 1, HEAD_W)], axis=-1)
            y = _lru_core(z, conv_w[j], conv_b[j][None], wcat, bcat, lru_lambda[j], n_batch=n_batch, seq=seq,
                          name=f"lru_{i}")
            x = _res_matmul(y, w_out_i, x, tm=512, tn=1024, name=f"mix_out_{i}")
        else:
            z = _norm_matmul(x, g_mix[i][None], w_in_i, tm=512, tn=1024, gelu_cols=2 * width, out_dtype=BF16,
                             name=f"mix_in_{i}")
            x = _gmlp_out(z, x, sg_norm[j][None], sg_w[j].astype(BF16), sg_b[j].T, w_out_i, tm=256,
                          name=f"gmlp_{i}")

        kv = _norm_matmul(mem, g_mem[i][None], xa_wkv[i].astype(BF16), tm=n_mem, tn=1024, gelu_cols=0,
                          out_dtype=BF16, name=f"mem_kv_{i}")
        x = _xattn(x, g_xattn[i][None], xa_wq[i].astype(BF16), kv, xa_wo[i].astype(BF16), seq=seq, n_mem=n_mem,
                   tm=512, name=f"xattn_{i}")
        x, ys_prev = _moe(x, g_moe[i][None], router_wg[i], router_bg[i], router_we[i], router_be[i],
                          moe_w_up, moe_w_down, ys_prev, layer=i)

    y_prompt = _final_norm(x, g_final[None], row0=0, n_rows=bp * seq, tm=512, name="final_prompt")
    y_sample = _final_norm(x, g_final[None], row0=bp * seq, n_rows=bs * seq, tm=512, name="final_sample")
    return (y_prompt.reshape(bp, seq, d), y_sample.reshape(bs, seq, d))
```
